```python
import jax, jax.numpy as jnp
from jax import lax
import numpy as np

D_MODEL = 1024
BATCH = 4
SEQ = 4096
DEPTH = 4
DEC_BATCH = 32
DEC_SEQ = 8
PAST_LEN = 8192
PAGE_SIZE = 128

N_MIXERS = 2
N_LRU = (DEPTH + 1) // 2
N_ATT = DEPTH // 2
D_RNN = (3 * D_MODEL) // 2
N_BLOCKS = 16
BLOCK_W = D_RNN // N_BLOCKS
CONV_W = 4
LRU_C = 8.0
N_HEADS = 16
HEAD_DIM = D_MODEL // N_HEADS
D_ATT = N_HEADS * HEAD_DIM
WINDOWS = (128, 512, 2048)
DILATIONS = (1, 4, 16)
N_GROUPS = 3
QKV_W = 3 * N_GROUPS * D_ATT
ATT_SCALE = HEAD_DIM ** -0.5
NEG_INF = -1e30
EPS = 1e-6

kernel_name = "hybrid_rglru_dilated_swa_adaln_step"


def rmsnorm(x, g):
    xf = x.astype(jnp.float32)
    y = xf * lax.rsqrt(jnp.mean(xf * xf, axis=-1, keepdims=True) + EPS)
    return (y * g.astype(jnp.float32)).astype(x.dtype)


def adaln(x, c, g, w, b):
    mod = jax.nn.silu(c) @ w + b
    shift, scale, gate = jnp.split(mod[:, None, :], 3, axis=-1)
    return rmsnorm(x, g) * (1 + scale) + shift, gate


def alibi_slopes():
    return 2.0 ** (-8.0 * jnp.arange(1, N_HEADS + 1, dtype=jnp.float32) / N_HEADS)


def lru_branch(h, conv_buf, h0, w_in, conv_w, conv_b, wa, ba, wx, bx, lam, w_out):
    B, T, _ = h.shape
    xb, gb = jnp.split(h @ w_in, 2, axis=-1)
    xpad = jnp.concatenate([conv_buf.astype(xb.dtype), xb], axis=1)
    xc = conv_b + sum(xpad[:, k:k + T] * conv_w[k] for k in range(CONV_W))
    new_buf = xpad[:, -(CONV_W - 1):]
    xblk = xc.reshape(B, T, N_BLOCKS, BLOCK_W)
    r = jax.nn.sigmoid(jnp.einsum('btni,nij->btnj', xblk, wa).reshape(B, T, D_RNN).astype(jnp.float32) + ba.astype(jnp.float32))
    ig = jax.nn.sigmoid(jnp.einsum('btni,nij->btnj', xblk, wx).reshape(B, T, D_RNN).astype(jnp.float32) + bx.astype(jnp.float32))
    log_a = -LRU_C * r * jax.nn.softplus(-lam.astype(jnp.float32))
    a = jnp.exp(log_a)
    u = jnp.sqrt(-jnp.expm1(2.0 * log_a)) * ig * xc.astype(jnp.float32)

    def step(hc, au):
        hc = au[0] * hc + au[1]
        return hc, hc

    hT, hs = lax.scan(step, h0.astype(jnp.float32), (a.transpose(1, 0, 2), u.transpose(1, 0, 2)))
    y = hs.transpose(1, 0, 2).astype(h.dtype) * jax.nn.silu(gb)
    return y @ w_out, new_buf, hT.astype(h0.dtype)


def dilated_attn_prompt(q, k, v, window, dil, slopes):
    B, S, H, hd = q.shape
    n = window // dil
    blk = n
    L = S // dil
    n_chunks = -(-L // blk)
    Lp = n_chunks * blk

    def to_res(t, front, back):
        t = t.reshape(B, L, dil, H, hd).transpose(0, 2, 1, 3, 4)
        return jnp.pad(t, ((0, 0), (0, 0), (front, back), (0, 0), (0, 0)))

    qr = to_res(q, 0, Lp - L).reshape(B, dil, n_chunks, blk, H, hd)
    kr = to_res(k, blk, Lp - L)
    vr = to_res(v, blk, Lp - L)

    def pairs(t):
        return jnp.concatenate([t[:, :, :Lp].reshape(B, dil, n_chunks, blk, H, hd),
                                t[:, :, blk:].reshape(B, dil, n_chunks, blk, H, hd)], axis=3)

    kp, vp = pairs(kr), pairs(vr)
    ii = jnp.arange(blk)[:, None]
    jj = jnp.arange(2 * blk)[None, :]
    diff = ii - jj + blk
    penalty = slopes[:, None, None] * (diff * dil).astype(jnp.float32)
    band = (diff >= 0) & (diff <= n)

    def one_chunk(args):
        qb, kb, vb, c = args
        s = jnp.einsum('brihd,brjhd->brhij', qb, kb).astype(jnp.float32) * ATT_SCALE - penalty
        valid = band & (c * blk - blk + jj >= 0)
        s = jnp.where(valid, s, NEG_INF)
        lse = jax.nn.logsumexp(s, axis=-1)
        p = jnp.exp(s - lse[..., None]).astype(vb.dtype)
        return jnp.einsum('brhij,brjhd->brihd', p, vb), lse

    mv = lambda t: jnp.moveaxis(t, 2, 0)
    o, lse = lax.map(one_chunk, (mv(qr), mv(kp), mv(vp), jnp.arange(n_chunks)))
    o = jnp.moveaxis(o, 0, 2).reshape(B, dil, Lp, H, hd)[:, :, :L]
    o = o.transpose(0, 2, 1, 3, 4).reshape(B, S, H, hd)
    lse = jnp.swapaxes(jnp.moveaxis(lse, 0, 2), -1, -2).reshape(B, dil, Lp, H)[:, :, :L]
    lse = lse.transpose(0, 2, 1, 3).reshape(B, S, H)
    return o, lse


def dilated_attn_sample(q, k_all, v_all, buf_len, window, dil, slopes):
    B, T, H, hd = q.shape
    n = window // dil
    steps = jnp.arange(n + 1)
    idx = buf_len + jnp.arange(T)[:, None] - dil * steps[None, :]
    valid = idx >= 0
    idx = jnp.maximum(idx, 0)
    kg = k_all[:, idx]
    vg = v_all[:, idx]
    s = jnp.einsum('bthd,btkhd->bhtk', q, kg).astype(jnp.float32) * ATT_SCALE \
        - slopes[:, None, None] * (dil * steps).astype(jnp.float32)
    s = jnp.where(valid, s, NEG_INF)
    lse = jax.nn.logsumexp(s, axis=-1)
    p = jnp.exp(s - lse[..., None]).astype(vg.dtype)
    o = jnp.einsum('bhtk,btkhd->bthd', p, vg)
    return o, lse.transpose(0, 2, 1)


def att_branch(h, w_in, w_out, bufs, slopes):
    B, T, _ = h.shape
    proj = h @ w_in
    qkv = proj[..., :QKV_W].reshape(B, T, N_GROUPS, 3, N_HEADS, HEAD_DIM)
    gate = proj[..., QKV_W:]
    outs, lses, new_kv = [], [], []
    for g in range(N_GROUPS):
        q, k, v = qkv[:, :, g, 0], qkv[:, :, g, 1], qkv[:, :, g, 2]
        if bufs is None:
            o, lse = dilated_attn_prompt(q, k, v, WINDOWS[g], DILATIONS[g], slopes)
            new_kv.append(jnp.stack([k, v], axis=2)[:, T - min(WINDOWS[g], T):])
        else:
            buf = bufs[g].astype(k.dtype)
            k_all = jnp.concatenate([buf[:, :, 0], k], axis=1)
            v_all = jnp.concatenate([buf[:, :, 1], v], axis=1)
            o, lse = dilated_attn_sample(q, k_all, v_all, buf.shape[1], WINDOWS[g], DILATIONS[g], slopes)
            new_kv.append(jnp.stack([k, v], axis=2))
        outs.append(o)
        lses.append(lse)
    wgt = jax.nn.softmax(jnp.stack(lses, axis=0), axis=0)
    o = jnp.einsum('gbth,gbthd->bthd', wgt, jnp.stack(outs, axis=0).astype(jnp.float32))
    y = o.reshape(B, T, D_ATT).astype(h.dtype) * jax.nn.silu(gate)
    return y @ w_out, new_kv


def setup_inputs(seed: int = 0) -> dict:
    key = jax.random.key(seed)
    ks = jax.random.split(key, 32)
    nrm = lambda k, shape, s: jax.random.normal(k, shape, jnp.float32) * s
    buf_lens = [min(w, PAST_LEN) for w in WINDOWS]
    lam = jax.random.uniform(ks[20], (N_LRU, D_RNN), jnp.float32, 0.9, 0.999)
    return {
        "x_prompt": nrm(ks[0], (BATCH, SEQ, D_MODEL), 1.0),
        "x_sample": nrm(ks[1], (DEC_BATCH, DEC_SEQ, D_MODEL), 1.0),
        "state_conv": nrm(ks[2], (N_LRU, DEC_BATCH, CONV_W - 1, D_RNN), 0.5),
        "state_h": nrm(ks[3], (N_LRU, DEC_BATCH, D_RNN), 0.5),
        "cache_kv_g0": nrm(ks[4], (N_ATT, DEC_BATCH, buf_lens[0], 2, N_HEADS, HEAD_DIM), 1.0),
        "cache_kv_g1": nrm(ks[5], (N_ATT, DEC_BATCH, buf_lens[1], 2, N_HEADS, HEAD_DIM), 1.0),
        "cache_kv_g2": nrm(ks[6], (N_ATT, DEC_BATCH, buf_lens[2], 2, N_HEADS, HEAD_DIM), 1.0),
        "c_prompt": nrm(ks[7], (BATCH, D_MODEL), 1.0),
        "c_sample": nrm(ks[8], (DEC_BATCH, D_MODEL), 1.0),
        "norm_g": 1.0 + nrm(ks[9], (DEPTH, D_MODEL), 0.02),
        "ada_w": nrm(ks[10], (DEPTH, D_MODEL, 3 * D_MODEL), 0.3 * D_MODEL ** -0.5),
        "ada_b": nrm(ks[11], (DEPTH, 3 * D_MODEL), 0.02),
        "final_g": 1.0 + nrm(ks[12], (D_MODEL,), 0.02),
        "lru_w_in": nrm(ks[13], (N_LRU, D_MODEL, 2 * D_RNN), D_MODEL ** -0.5),
        "lru_conv_w": nrm(ks[14], (N_LRU, CONV_W, D_RNN), CONV_W ** -0.5),
        "lru_conv_b": nrm(ks[15], (N_LRU, D_RNN), 0.02),
        "lru_wa": nrm(ks[16], (N_LRU, N_BLOCKS, BLOCK_W, BLOCK_W), BLOCK_W ** -0.5),
        "lru_ba": nrm(ks[17], (N_LRU, D_RNN), 0.02),
        "lru_wx": nrm(ks[18], (N_LRU, N_BLOCKS, BLOCK_W, BLOCK_W), BLOCK_W ** -0.5),
        "lru_bx": nrm(ks[19], (N_LRU, D_RNN), 0.02),
        "lru_lambda": jnp.log(lam) - jnp.log1p(-lam),
        "lru_w_out": nrm(ks[21], (N_LRU, D_RNN, D_MODEL), D_RNN ** -0.5),
        "att_w_in": nrm(ks[22], (N_ATT, D_MODEL, QKV_W + D_ATT), D_MODEL ** -0.5),
        "att_w_out": nrm(ks[23], (N_ATT, D_ATT, D_MODEL), D_ATT ** -0.5),
    }


def reference(x_prompt, x_sample, state_conv, state_h, cache_kv_g0, cache_kv_g1, cache_kv_g2,
              c_prompt, c_sample, norm_g, ada_w, ada_b, final_g,
              lru_w_in, lru_conv_w, lru_conv_b, lru_wa, lru_ba, lru_wx, lru_bx, lru_lambda, lru_w_out,
              att_w_in, att_w_out):
    slopes = alibi_slopes()
    yp, ys = x_prompt, x_sample
    Bp = x_prompt.shape[0]
    p_conv, p_h, s_conv, s_h = [], [], [], []
    p_kv = [[] for _ in range(N_GROUPS)]
    s_kv = [[] for _ in range(N_GROUPS)]
    for i in range(DEPTH):
        hp, gp = adaln(yp, c_prompt, norm_g[i], ada_w[i], ada_b[i])
        hs, gs = adaln(ys, c_sample, norm_g[i], ada_w[i], ada_b[i])
        j = i // N_MIXERS
        if i % N_MIXERS == 0:
            prm = (lru_w_in[j], lru_conv_w[j], lru_conv_b[j], lru_wa[j], lru_ba[j],
                   lru_wx[j], lru_bx[j], lru_lambda[j], lru_w_out[j])
            zbuf = jnp.zeros((Bp, CONV_W - 1, D_RNN), x_prompt.dtype)
            zh = jnp.zeros((Bp, D_RNN), x_prompt.dtype)
            op, bp, hTp = lru_branch(hp, zbuf, zh, *prm)
            os_, bs, hTs = lru_branch(hs, state_conv[j], state_h[j], *prm)
            p_conv.append(bp); p_h.append(hTp); s_conv.append(bs); s_h.append(hTs)
        else:
            op, kvp = att_branch(hp, att_w_in[j], att_w_out[j], None, slopes)
            os_, kvs = att_branch(hs, att_w_in[j], att_w_out[j],
                                  (cache_kv_g0[j], cache_kv_g1[j], cache_kv_g2[j]), slopes)
            for g in range(N_GROUPS):
                p_kv[g].append(kvp[g]); s_kv[g].append(kvs[g])
        yp = yp + gp * op
        ys = ys + gs * os_
    y_prompt = rmsnorm(yp, final_g)
    y_sample = rmsnorm(ys, final_g)
    return (y_prompt, y_sample,
            jnp.stack(p_conv), jnp.stack(p_h),
            jnp.stack(p_kv[0]), jnp.stack(p_kv[1]), jnp.stack(p_kv[2]),
            jnp.stack(s_conv), jnp.stack(s_h),
            jnp.stack(s_kv[0]), jnp.stack(s_kv[1]), jnp.stack(s_kv[2]))
```

```python
import functools
import math

import jax
import jax.numpy as jnp
from jax import lax
from jax.experimental import pallas as pl
from jax.experimental.pallas import tpu as pltpu

F32 = jnp.float32
BF16 = jnp.bfloat16

DEPTH = 4
N_MIXERS = 2
N_BLOCKS = 16
CONV_W = 4
LRU_C = 8.0
N_HEADS = 16
WINDOWS = (128, 512, 2048)
DILATIONS = (1, 4, 16)
N_GROUPS = 3
NEG_INF = -1e30
EPS = 1e-6

LANES = 128
SUBLANES = 8
VMEM_LIMIT = 56 * 1024 * 1024
PROMPT_TILE = 256
GATE_GROUP = 4
N_CHUNK = 1024


def _silu(x):
    return x * jax.nn.sigmoid(x)


def _adaln_norm(x, g, shift, scale):
    y = x * lax.rsqrt(jnp.mean(x * x, axis=-1, keepdims=True) + EPS)
    return (y * g) * (1.0 + scale) + shift


def _const_spec(shape):
    nd = len(shape)
    return pl.BlockSpec(shape, lambda *_: (0,) * nd, pipeline_mode=pl.Buffered(1))


def _params(n_grid):
    return pltpu.CompilerParams(dimension_semantics=("arbitrary",) * n_grid,
                                vmem_limit_bytes=VMEM_LIMIT)


def _mod_spec(mod, nb, t):
    _, mt, d3 = mod.shape
    if mt == 1:
        return pl.BlockSpec((nb, 1, d3), lambda i, j: (i, 0, 0))
    return pl.BlockSpec((nb, t, d3), lambda i, j: (i, j, 0))


def _mod_body(c_ref, w_ref, b_ref, o_ref):
    a = _silu(c_ref[...]).astype(BF16)
    o_ref[...] = jnp.dot(a, w_ref[...].astype(BF16), preferred_element_type=F32) + b_ref[...]


def _adaln_mod(c_all, ada_w, ada_b):
    r, d = c_all.shape
    depth, _, d3 = ada_w.shape
    tn = N_CHUNK
    return pl.pallas_call(
        _mod_body,
        grid=(depth, d3 // tn),
        in_specs=[pl.BlockSpec((r, d), lambda i, n: (0, 0)),
                  pl.BlockSpec((None, d, tn), lambda i, n: (i, 0, n)),
                  pl.BlockSpec((None, 1, tn), lambda i, n: (i, 0, n))],
        out_specs=pl.BlockSpec((None, r, tn), lambda i, n: (i, 0, n)),
        out_shape=jax.ShapeDtypeStruct((depth, r, d3), F32),
        compiler_params=_params(2),
        name="adaln_mod",
    )(c_all, ada_w, ada_b.reshape(depth, 1, d3))


def _lru_body(x_ref, mod_ref, g_ref, win_ref, cw_ref, cb_ref, wg_ref, ba_ref, bx_ref, lam_ref,
              wout_ref, cs0_ref, h0_ref, y_ref, cs_ref, ht_ref, prev_sc, h_sc):
    nb, t, d = x_ref.shape
    r = cw_ref.shape[1]
    gw = wg_ref.shape[1]

    @pl.when(pl.program_id(1) == 0)
    def _():
        prev_sc[...] = cs0_ref[...]
        h_sc[...] = h0_ref[...]

    x = x_ref[...]
    mod = mod_ref[...]
    shift, scale, gate = mod[..., :d], mod[..., d:2 * d], mod[..., 2 * d:]
    hn = _adaln_norm(x, g_ref[...], shift, scale).reshape(nb * t, d).astype(BF16)
    proj = jnp.dot(hn, win_ref[...], preferred_element_type=F32)
    xb = proj[:, :r].reshape(nb, t, r)
    gb = proj[:, r:]

    cat = jnp.concatenate([prev_sc[...], xb], axis=1)
    cw = cw_ref[...]
    xc = cb_ref[...] + xb * cw[CONV_W - 1:CONV_W]
    for k in range(1, CONV_W):
        xc = xc + pltpu.roll(cat, k, 1)[:, SUBLANES:, :] * cw[CONV_W - 1 - k:CONV_W - k]
    new_prev = cat[:, t:, :]
    prev_sc[...] = new_prev
    cs_ref[...] = new_prev

    xc2 = xc.reshape(nb * t, r)
    xcb = xc2.astype(BF16)
    grs, gis = [], []
    for q in range(r // gw):
        p = jnp.dot(xcb[:, q * gw:(q + 1) * gw], wg_ref[q], preferred_element_type=F32)
        grs.append(p[:, :gw])
        gis.append(p[:, gw:])
    rg = jax.nn.sigmoid(jnp.concatenate(grs, axis=1) + ba_ref[...])
    ig = jax.nn.sigmoid(jnp.concatenate(gis, axis=1) + bx_ref[...])
    nlam = -lam_ref[...]
    softplus = jnp.maximum(nlam, 0.0) + jnp.log1p(jnp.exp(-jnp.abs(nlam)))
    log_a = (-LRU_C) * rg * softplus
    a = jnp.exp(log_a)
    em1 = a - 1.0
    em1 = jnp.where(a == 1.0, log_a, jnp.where(a <= 0.5, em1, em1 * log_a / jnp.log(a)))
    u = jnp.sqrt(-em1 * (1.0 + a)) * ig * xc2

    a3 = a.reshape(nb, t, r)
    u3 = u.reshape(nb, t, r)
    row = lax.broadcasted_iota(jnp.int32, (nb, t, r), 1)
    dist = 1
    while dist < t:
        keep = row >= dist
        u3 = jnp.where(keep, a3 * pltpu.roll(u3, dist, 1) + u3, u3)
        a3 = jnp.where(keep, a3 * pltpu.roll(a3, dist, 1), a3)
        dist *= 2
    h = a3 * h_sc[...] + u3
    h_last = h[:, t - 1:t, :]
    h_sc[...] = h_last
    ht_ref[...] = h_last

    y = (h.reshape(nb * t, r) * _silu(gb)).astype(BF16)
    out = jnp.dot(y, wout_ref[...], preferred_element_type=F32).reshape(nb, t, d)
    y_ref[...] = x + gate * out


def _gate_weights(wa, wx):
    nblk, bw, _ = wa.shape
    ngrp = nblk // GATE_GROUP
    eye = jnp.eye(GATE_GROUP, dtype=wa.dtype)[None, :, None, :, None]

    def block_diag(w):
        w = w.reshape(ngrp, GATE_GROUP, bw, 1, bw) * eye
        return w.reshape(ngrp, GATE_GROUP * bw, GATE_GROUP * bw)

    return jnp.concatenate([block_diag(wa), block_diag(wx)], axis=-1).astype(BF16)


def _lru_layer(x, mod, norm_g, w_in, conv_w, conv_b, wg, ba, bx, lam, w_out, cs0, h0, nb, t):
    b, s, d = x.shape
    r = conv_w.shape[1]
    gw = wg.shape[1]
    row = lambda v: v.reshape(1, -1)
    blk = lambda i, j: (i, j, 0)
    per_b = lambda i, j: (i, 0, 0)
    return pl.pallas_call(
        _lru_body,
        grid=(b // nb, s // t),
        in_specs=[pl.BlockSpec((nb, t, d), blk),
                  pl.BlockSpec((nb, 1, 3 * d), per_b),
                  _const_spec((1, d)),
                  _const_spec((d, 2 * r)),
                  _const_spec((CONV_W, r)),
                  _const_spec((1, r)),
                  _const_spec((r // gw, gw, 2 * gw)),
                  _const_spec((1, r)),
                  _const_spec((1, r)),
                  _const_spec((1, r)),
                  _const_spec((r, d)),
                  pl.BlockSpec((nb, SUBLANES, r), per_b),
                  pl.BlockSpec((nb, 1, r), per_b)],
        out_specs=[pl.BlockSpec((nb, t, d), blk),
                   pl.BlockSpec((nb, SUBLANES, r), per_b),
                   pl.BlockSpec((nb, 1, r), per_b)],
        out_shape=[jax.ShapeDtypeStruct((b, s, d), F32),
                   jax.ShapeDtypeStruct((b, SUBLANES, r), F32),
                   jax.ShapeDtypeStruct((b, 1, r), F32)],
        scratch_shapes=[pltpu.VMEM((nb, SUBLANES, r), F32), pltpu.VMEM((nb, 1, r), F32)],
        compiler_params=_params(2),
        name="lru_layer",
    )(x, mod, row(norm_g), w_in, conv_w, row(conv_b), wg, row(ba), row(bx), row(lam), w_out, cs0, h0)


def _attproj_body(x_ref, mod_ref, g_ref, w_ref, p_ref, t0_ref, t1_ref, t2_ref, *, att_scale):
    nb, t, d = x_ref.shape
    mod = mod_ref[...]
    hn = _adaln_norm(x_ref[...], g_ref[...], mod[..., :d], mod[..., d:2 * d])
    hn = hn.reshape(nb * t, d).astype(BF16)
    tails = (t0_ref, t1_ref, t2_ref)
    for c in range(3 * N_GROUPS + 1):
        cols = slice(c * N_CHUNK, (c + 1) * N_CHUNK)
        res = jnp.dot(hn, w_ref[:, cols], preferred_element_type=F32).reshape(nb, t, N_CHUNK)
        g, j = divmod(c, 3)
        if g < N_GROUPS and j == 0:
            res = res * att_scale
        p_ref[:, :, cols] = res.astype(p_ref.dtype)
        if g < N_GROUPS and j > 0:
            rows = tails[g].shape[1]
            tails[g][:, :, (j - 1) * N_CHUNK:j * N_CHUNK] = res[:, t - rows:, :]


def _att_proj(x, mod, norm_g, w_in, nb, t, tail_lens, p_dtype):
    b, s, d = x.shape
    ncol = w_in.shape[1]
    n_t = s // t
    tail_shapes, tail_specs = [], []
    for g in range(N_GROUPS):
        wlen = tail_lens[g]
        rows = min(wlen, t)
        first = n_t - wlen // rows
        tail_shapes.append(jax.ShapeDtypeStruct((b, wlen, 2 * N_CHUNK), F32))
        tail_specs.append(pl.BlockSpec((nb, rows, 2 * N_CHUNK),
                                       lambda i, j, first=first: (i, jnp.maximum(j - first, 0), 0)))
    head_dim = d // N_HEADS
    return pl.pallas_call(
        functools.partial(_attproj_body, att_scale=head_dim ** -0.5),
        grid=(b // nb, n_t),
        in_specs=[pl.BlockSpec((nb, t, d), lambda i, j: (i, j, 0)),
                  _mod_spec(mod, nb, t),
                  _const_spec((1, d)),
                  _const_spec((d, ncol))],
        out_specs=[pl.BlockSpec((nb, t, ncol), lambda i, j: (i, j, 0))] + tail_specs,
        out_shape=[jax.ShapeDtypeStruct((b, s, ncol), p_dtype)] + tail_shapes,
        compiler_params=_params(2),
        name="att_proj",
    )(x, mod, norm_g.reshape(1, -1), w_in)


def _slope(h):
    return 2.0 ** (-8.0 * (h + 1) / N_HEADS)


def _pattn_body(q_ref, kp_ref, kc_ref, vp_ref, vc_ref, o_ref, lse_ref, *, dil):
    blk = q_ref.shape[0]
    c = pl.program_id(2)
    ii = lax.broadcasted_iota(jnp.int32, (blk, 2 * blk), 0)
    jj = lax.broadcasted_iota(jnp.int32, (blk, 2 * blk), 1)
    diff = ii - jj + blk
    valid = (diff >= 0) & (diff <= blk) & ((c > 0) | (jj >= blk))
    dist = (diff * dil).astype(F32)
    lane = lax.broadcasted_iota(jnp.int32, (blk, LANES), 1)
    low = lane < (LANES // 2)
    lse_all = jnp.zeros((blk, LANES), F32)
    for p in range(N_HEADS // 2):
        cols = slice(p * LANES, (p + 1) * LANES)
        q2 = q_ref[:, cols]
        k2 = jnp.concatenate([kp_ref[:, cols], kc_ref[:, cols]], axis=0)
        v2 = jnp.concatenate([vp_ref[:, cols], vc_ref[:, cols]], axis=0)
        outs = []
        for half in range(2):
            h = 2 * p + half
            qm = jnp.where(low if half == 0 else ~low, q2, jnp.zeros_like(q2))
            s = lax.dot_general(qm, k2, (((1,), (1,)), ((), ())), preferred_element_type=F32)
            s = jnp.where(valid, s - _slope(h) * dist, NEG_INF)
            m = jnp.max(s, axis=-1, keepdims=True)
            e = jnp.exp(s - m)
            l = jnp.sum(e, axis=-1, keepdims=True)
            outs.append(jnp.dot(e.astype(BF16), v2, preferred_element_type=F32) / l)
            lse_all = jnp.where(lane == h, m + jnp.log(l), lse_all)
        o_ref[:, cols] = jnp.where(low, outs[0], outs[1]).astype(o_ref.dtype)
    lse_ref[...] = lse_all


def _prompt_attn(p, g):
    b, s, ncol = p.shape
    dil, n = DILATIONS[g], WINDOWS[g] // DILATIONS[g]
    steps = s // dil
    n_chunks = steps // n
    nchunkcols = ncol // N_CHUNK
    pv = p.reshape(b, steps, dil * ncol)
    qcol = lambda r: r * nchunkcols + 3 * g
    cur = lambda off: (lambda i, r, c: (i, c, qcol(r) + off))
    prev = lambda off: (lambda i, r, c: (i, jnp.maximum(c - 1, 0), qcol(r) + off))
    spec = lambda im: pl.BlockSpec((None, n, N_CHUNK), im)
    o, lse = pl.pallas_call(
        functools.partial(_pattn_body, dil=dil),
        grid=(b, dil, n_chunks),
        in_specs=[spec(cur(0)), spec(prev(1)), spec(cur(1)), spec(prev(2)), spec(cur(2))],
        out_specs=[pl.BlockSpec((None, n, N_CHUNK), lambda i, r, c: (i, c, r)),
                   pl.BlockSpec((None, n, LANES), lambda i, r, c: (i, c, r))],
        out_shape=[jax.ShapeDtypeStruct((b, steps, dil * N_CHUNK), BF16),
                   jax.ShapeDtypeStruct((b, steps, dil * LANES), F32)],
        compiler_params=_params(3),
        name=f"prompt_attn_g{g}",
    )(pv, pv, pv, pv, pv)
    return o.reshape(b, s, N_CHUNK), lse.reshape(b, s, LANES)


def _sattn_body(p_ref, c0_ref, c1_ref, c2_ref, o0_ref, o1_ref, o2_ref, l0_ref, l1_ref, l2_ref):
    t = p_ref.shape[0]
    hd = N_CHUNK // N_HEADS
    hrow = lax.broadcasted_iota(jnp.int32, (N_HEADS, N_CHUNK), 0)
    hcol = lax.broadcasted_iota(jnp.int32, (N_HEADS, N_CHUNK), 1)
    own_head = hrow == hcol // hd
    eye = (lax.broadcasted_iota(jnp.int32, (N_HEADS, LANES), 0)
           == lax.broadcasted_iota(jnp.int32, (N_HEADS, LANES), 1))
    trow_o = lax.broadcasted_iota(jnp.int32, (t, N_CHUNK), 0)
    trow_l = lax.broadcasted_iota(jnp.int32, (t, LANES), 0)
    nt = (((1,), (1,)), ((), ()))
    caches = (c0_ref, c1_ref, c2_ref)
    o_refs = (o0_ref, o1_ref, o2_ref)
    l_refs = (l0_ref, l1_ref, l2_ref)
    for g in range(N_GROUPS):
        dil = DILATIONS[g]
        n = caches[g].shape[0]
        shift = int(math.log2(dil))
        q = p_ref[:, (3 * g) * N_CHUNK:(3 * g + 1) * N_CHUNK]
        kn = p_ref[:, (3 * g + 1) * N_CHUNK:(3 * g + 2) * N_CHUNK]
        vn = p_ref[:, (3 * g + 2) * N_CHUNK:(3 * g + 3) * N_CHUNK]
        pad = jnp.zeros((LANES - t, N_CHUNK), F32)
        kn = jnp.concatenate([kn, pad], axis=0).astype(BF16)
        vn = jnp.concatenate([vn, pad], axis=0).astype(BF16)
        o_acc = jnp.zeros((t, N_CHUNK), F32)
        l_acc = jnp.zeros((t, LANES), F32)
        for rho in range(min(dil, t)):
            ts = list(range(rho, t, dil))
            rows = len(ts) * N_HEADS
            qbd = jnp.concatenate(
                [jnp.where(own_head, jnp.broadcast_to(q[ti:ti + 1, :], (N_HEADS, N_CHUNK)), 0.0)
                 for ti in ts], axis=0).astype(BF16)
            base = rho * 2 * N_CHUNK
            kb = caches[g][:, base:base + N_CHUNK].astype(BF16)
            vb = caches[g][:, base + N_CHUNK:base + 2 * N_CHUNK].astype(BF16)
            sb = lax.dot_general(qbd, kb, nt, preferred_element_type=F32)
            sn = lax.dot_general(qbd, kn, nt, preferred_element_type=F32)
            ri = lax.broadcasted_iota(jnp.int32, (rows, 1), 0)
            qi = ri // N_HEADS
            head = ri - qi * N_HEADS
            slope = jnp.exp((head + 1).astype(F32) * (-8.0 / N_HEADS * math.log(2.0)))
            tq = rho + dil * qi
            jb = lax.broadcasted_iota(jnp.int32, (rows, n), 1)
            step_b = n - jb + qi
            sb = jnp.where(step_b <= n, sb - slope * (dil * step_b).astype(F32), NEG_INF)
            dn = tq - lax.broadcasted_iota(jnp.int32, (rows, LANES), 1)
            ok_n = (dn >= 0) & ((dn & (dil - 1)) == 0)
            sn = jnp.where(ok_n, sn - slope * dn.astype(F32), NEG_INF)
            m = jnp.maximum(jnp.max(sb, axis=-1, keepdims=True), jnp.max(sn, axis=-1, keepdims=True))
            eb = jnp.exp(sb - m)
            en = jnp.exp(sn - m)
            l = jnp.sum(eb, axis=-1, keepdims=True) + jnp.sum(en, axis=-1, keepdims=True)
            o = (jnp.dot(eb.astype(BF16), vb, preferred_element_type=F32)
                 + jnp.dot(en.astype(BF16), vn, preferred_element_type=F32)) / l
            lse = m + jnp.log(l)
            for i, ti in enumerate(ts):
                blk = slice(i * N_HEADS, (i + 1) * N_HEADS)
                orow = jnp.sum(jnp.where(own_head, o[blk], 0.0), axis=0, keepdims=True)
                lrow = jnp.sum(jnp.where(eye, lse[blk], 0.0), axis=0, keepdims=True)
                o_acc = jnp.where(trow_o == ti, orow, o_acc)
                l_acc = jnp.where(trow_l == ti, lrow, l_acc)
        o_refs[g][...] = o_acc.astype(o_refs[g].dtype)
        l_refs[g][...] = l_acc


def _sample_attn(p, caches, layer):
    b, t, ncol = p.shape
    in_specs = [pl.BlockSpec((None, t, ncol), lambda i: (i, 0, 0))]
    views = []
    for g in range(N_GROUPS):
        dil = DILATIONS[g]
        natt, _, blen = caches[g].shape[:3]
        n = blen // dil
        views.append(caches[g].reshape(natt, b, n, dil * 2 * N_CHUNK))
        used = min(dil, t) * 2 * N_CHUNK
        in_specs.append(pl.BlockSpec((None, None, n, used), lambda i, layer=layer: (layer, i, 0, 0)))
    outs = pl.pallas_call(
        _sattn_body,
        grid=(b,),
        in_specs=in_specs,
        out_specs=[pl.BlockSpec((None, t, N_CHUNK), lambda i: (i, 0, 0))] * N_GROUPS
                  + [pl.BlockSpec((None, t, LANES), lambda i: (i, 0, 0))] * N_GROUPS,
        out_shape=[jax.ShapeDtypeStruct((b, t, N_CHUNK), F32)] * N_GROUPS
                  + [jax.ShapeDtypeStruct((b, t, LANES), F32)] * N_GROUPS,
        compiler_params=_params(1),
        name="sample_attn",
    )(p, *views)
    return outs[:N_GROUPS], outs[N_GROUPS:]


def _attout_body(x_ref, mod_ref, gate_ref, o0_ref, o1_ref, o2_ref, l0_ref, l1_ref, l2_ref,
                 wout_ref, fg_ref, y_ref, *, final_norm):
    nb, t, d = x_ref.shape
    rows = nb * t
    hd = d // N_HEADS
    lses = [r[...].reshape(rows, LANES) for r in (l0_ref, l1_ref, l2_ref)]
    m = jnp.maximum(jnp.maximum(lses[0], lses[1]), lses[2])
    es = [jnp.exp(l - m) for l in lses]
    den = es[0] + es[1] + es[2]
    expand = (lax.broadcasted_iota(jnp.int32, (LANES, d), 0)
              == lax.broadcasted_iota(jnp.int32, (LANES, d), 1) // hd).astype(BF16)
    acc = jnp.zeros((rows, d), F32)
    for e, o_ref in zip(es, (o0_ref, o1_ref, o2_ref)):
        w = e / den
        hi = w.astype(BF16)
        lo = (w - hi.astype(F32)).astype(BF16)
        wfull = (jnp.dot(hi, expand, preferred_element_type=F32)
                 + jnp.dot(lo, expand, preferred_element_type=F32))
        acc = acc + wfull * o_ref[...].reshape(rows, d).astype(F32)
    y = (acc * _silu(gate_ref[...].reshape(rows, d).astype(F32))).astype(BF16)
    out = jnp.dot(y, wout_ref[...], preferred_element_type=F32).reshape(nb, t, d)
    xn = x_ref[...] + mod_ref[...][..., 2 * d:] * out
    if final_norm:
        xn = xn * lax.rsqrt(jnp.mean(xn * xn, axis=-1, keepdims=True) + EPS) * fg_ref[...]
    y_ref[...] = xn


def _att_out(x, mod, p, os_, lses, w_out, final_g, final_norm, nb, t):
    b, s, d = x.shape
    gate_col = p.shape[2] // N_CHUNK - 1
    blk = lambda i, j: (i, j, 0)
    return pl.pallas_call(
        functools.partial(_attout_body, final_norm=final_norm),
        grid=(b // nb, s // t),
        in_specs=[pl.BlockSpec((nb, t, d), blk),
                  _mod_spec(mod, nb, t),
                  pl.BlockSpec((nb, t, N_CHUNK), lambda i, j: (i, j, gate_col))]
                 + [pl.BlockSpec((nb, t, N_CHUNK), blk)] * N_GROUPS
                 + [pl.BlockSpec((nb, t, LANES), blk)] * N_GROUPS
                 + [_const_spec((d, d)), _const_spec((1, d))],
        out_specs=pl.BlockSpec((nb, t, d), blk),
        out_shape=jax.ShapeDtypeStruct((b, s, d), F32),
        compiler_params=_params(2),
        name="att_out",
    )(x, mod, p, *os_, *lses, w_out, final_g.reshape(1, -1))


def kernel(x_prompt, x_sample, state_conv, state_h, cache_kv_g0, cache_kv_g1, cache_kv_g2, c_prompt, c_sample, norm_g, ada_w, ada_b, final_g, lru_w_in, lru_conv_w, lru_conv_b, lru_wa, lru_ba, lru_wx, lru_bx, lru_lambda, lru_w_out, att_w_in, att_w_out):
    bp, sp, d = x_prompt.shape
    bs, ts, _ = x_sample.shape
    r = lru_conv_w.shape[-1]
    hd = d // N_HEADS
    caches = (cache_kv_g0, cache_kv_g1, cache_kv_g2)

    n_c = bp + bs
    c_all = jnp.concatenate([c_prompt, c_sample, jnp.zeros((-n_c % 16, d), F32)], axis=0)
    mod = _adaln_mod(c_all, ada_w, ada_b)
    mod_p = mod[:, :bp].reshape(DEPTH, bp, 1, 3 * d)
    mod_s = mod[:, bp:n_c].reshape(DEPTH, bs, 1, 3 * d)

    yp, ys = x_prompt, x_sample
    p_conv, p_h, s_conv, s_h = [], [], [], []
    p_kv = [[] for _ in range(N_GROUPS)]
    s_kv = [[] for _ in range(N_GROUPS)]
    n_keep = CONV_W - 1
    for i in range(DEPTH):
        j = i // N_MIXERS
        if i % N_MIXERS == 0:
            wg = _gate_weights(lru_wa[j], lru_wx[j])
            args = (norm_g[i], lru_w_in[j].astype(BF16), lru_conv_w[j], lru_conv_b[j], wg, lru_ba[j],
                    lru_bx[j], lru_lambda[j], lru_w_out[j].astype(BF16))
            zc = jnp.zeros((bp, SUBLANES, r), F32)
            zh = jnp.zeros((bp, 1, r), F32)
            yp, cs, ht = _lru_layer(yp, mod_p[i], *args, zc, zh, 1, PROMPT_TILE)
            p_conv.append(cs[:, SUBLANES - n_keep:])
            p_h.append(ht[:, 0])
            cs0 = jnp.pad(state_conv[j], ((0, 0), (SUBLANES - n_keep, 0), (0, 0)))
            ys, cs, ht = _lru_layer(ys, mod_s[i], *args, cs0, state_h[j][:, None, :], bs, ts)
            s_conv.append(cs[:, SUBLANES - n_keep:])
            s_h.append(ht[:, 0])
        else:
            w_in = att_w_in[j].astype(BF16)
            w_out = att_w_out[j].astype(BF16)
            last = i == DEPTH - 1
            p_tails = [min(w, sp) for w in WINDOWS]
            pp, *tails = _att_proj(yp, mod_p[i], norm_g[i], w_in, 1, PROMPT_TILE, p_tails, BF16)
            res = [_prompt_attn(pp, g) for g in range(N_GROUPS)]
            yp = _att_out(yp, mod_p[i], pp, [o for o, _ in res], [l for _, l in res], w_out, final_g,
                          last, 1, PROMPT_TILE)
            for g in range(N_GROUPS):
                p_kv[g].append(tails[g].reshape(bp, -1, 2, N_HEADS, hd))
            n_s = bs * ts
            flat = lambda v: v.reshape(1, n_s, v.shape[-1])
            mod_rows = flat(jnp.broadcast_to(mod_s[i], (bs, ts, 3 * d)))
            ps, *tails = _att_proj(flat(ys), mod_rows, norm_g[i], w_in, 1, n_s, [n_s] * N_GROUPS, F32)
            os_, lses = _sample_attn(ps.reshape(bs, ts, -1), caches, j)
            ys = _att_out(flat(ys), mod_rows, ps, [flat(o) for o in os_], [flat(l) for l in lses],
                          w_out, final_g, last, 1, n_s).reshape(bs, ts, d)
            for g in range(N_GROUPS):
                s_kv[g].append(tails[g].reshape(bs, ts, 2, N_HEADS, hd))
    return (yp, ys,
            jnp.stack(p_conv), jnp.stack(p_h),
            jnp.stack(p_kv[0]), jnp.stack(p_kv[1]), jnp.stack(p_kv[2]),
            jnp.stack(s_conv), jnp.stack(s_h),
            jnp.stack(s_kv[0]), jnp.stack(s_kv[1]), jnp.stack(s_kv[2]))
```

```python
import functools
import math

import jax
import jax.numpy as jnp
from jax import lax
from jax.experimental import pallas as pl
from jax.experimental.pallas import tpu as pltpu

F32 = jnp.float32
BF16 = jnp.bfloat16

DEPTH = 4
N_MIXERS = 2
N_BLOCKS = 16
CONV_W = 4
LRU_C = 8.0
N_HEADS = 16
WINDOWS = (128, 512, 2048)
DILATIONS = (1, 4, 16)
N_GROUPS = 3
NEG_INF = -1e30
EPS = 1e-6

LANES = 128
SUBLANES = 8
VMEM_LIMIT = 56 * 1024 * 1024
PROMPT_TILE = 256
GATE_GROUP = 4
N_CHUNK = 1024
SAMPLE_POS_CHUNK = 1024
ATTN_QUERY_BLOCK = 512
MASK_DIST = 1e33


def _silu(x):
    return x * jax.nn.sigmoid(x)


def _adaln_norm(x, g, shift, scale):
    y = x * lax.rsqrt(jnp.mean(x * x, axis=-1, keepdims=True) + EPS)
    return (y * g) * (1.0 + scale) + shift


def _const_spec(shape):
    nd = len(shape)
    return pl.BlockSpec(shape, lambda *_: (0,) * nd, pipeline_mode=pl.Buffered(1))


def _params(n_grid):
    return pltpu.CompilerParams(dimension_semantics=("arbitrary",) * n_grid,
                                vmem_limit_bytes=VMEM_LIMIT)


def _mod_spec(mod, nb, t):
    _, mt, d3 = mod.shape
    if mt == 1:
        return pl.BlockSpec((nb, 1, d3), lambda i, j: (i, 0, 0))
    return pl.BlockSpec((nb, t, d3), lambda i, j: (i, j, 0))


def _mod_body(c_ref, w_ref, b_ref, o_ref):
    a = _silu(c_ref[...]).astype(BF16)
    o_ref[...] = jnp.dot(a, w_ref[...].astype(BF16), preferred_element_type=F32) + b_ref[...]


def _adaln_mod(c_all, ada_w, ada_b):
    r, d = c_all.shape
    depth, _, d3 = ada_w.shape
    tn = N_CHUNK
    return pl.pallas_call(
        _mod_body,
        grid=(depth, d3 // tn),
        in_specs=[pl.BlockSpec((r, d), lambda i, n: (0, 0)),
                  pl.BlockSpec((None, d, tn), lambda i, n: (i, 0, n)),
                  pl.BlockSpec((None, 1, tn), lambda i, n: (i, 0, n))],
        out_specs=pl.BlockSpec((None, r, tn), lambda i, n: (i, 0, n)),
        out_shape=jax.ShapeDtypeStruct((depth, r, d3), F32),
        compiler_params=_params(2),
        name="adaln_mod",
    )(c_all, ada_w, ada_b.reshape(depth, 1, d3))


def _lru_body(x_ref, mod_ref, g_ref, win_ref, cw_ref, cb_ref, wg_ref, ba_ref, bx_ref, lam_ref,
              wout_ref, cs0_ref, h0_ref, y_ref, cs_ref, ht_ref, prev_sc, h_sc):
    nb, t, d = x_ref.shape
    r = cw_ref.shape[1]
    gw = wg_ref.shape[1]

    @pl.when(pl.program_id(1) == 0)
    def _():
        prev_sc[...] = cs0_ref[...]
        h_sc[...] = h0_ref[...]

    x = x_ref[...]
    mod = mod_ref[...]
    shift, scale, gate = mod[..., :d], mod[..., d:2 * d], mod[..., 2 * d:]
    hn = _adaln_norm(x, g_ref[...], shift, scale).reshape(nb * t, d).astype(BF16)
    proj = jnp.dot(hn, win_ref[...], preferred_element_type=F32)
    xb = proj[:, :r].reshape(nb, t, r)
    gb = proj[:, r:]

    cat = jnp.concatenate([prev_sc[...], xb], axis=1)
    cw = cw_ref[...]
    xc = cb_ref[...] + xb * cw[CONV_W - 1:CONV_W]
    for k in range(1, CONV_W):
        xc = xc + pltpu.roll(cat, k, 1)[:, SUBLANES:, :] * cw[CONV_W - 1 - k:CONV_W - k]
    new_prev = cat[:, t:, :]
    prev_sc[...] = new_prev
    cs_ref[...] = new_prev

    xc2 = xc.reshape(nb * t, r)
    xcb = xc2.astype(BF16)
    grs, gis = [], []
    for q in range(r // gw):
        p = jnp.dot(xcb[:, q * gw:(q + 1) * gw], wg_ref[q], preferred_element_type=F32)
        grs.append(p[:, :gw])
        gis.append(p[:, gw:])
    rg = jax.nn.sigmoid(jnp.concatenate(grs, axis=1) + ba_ref[...])
    ig = jax.nn.sigmoid(jnp.concatenate(gis, axis=1) + bx_ref[...])
    nlam = -lam_ref[...]
    softplus = jnp.maximum(nlam, 0.0) + jnp.log1p(jnp.exp(-jnp.abs(nlam)))
    log_a = (-LRU_C) * rg * softplus
    a = jnp.exp(log_a)
    em1 = a - 1.0
    em1 = jnp.where(a == 1.0, log_a, jnp.where(a <= 0.5, em1, em1 * log_a / jnp.log(a)))
    u = jnp.sqrt(-em1 * (1.0 + a)) * ig * xc2

    groups = t // SUBLANES
    a3 = a.reshape(nb * groups, SUBLANES, r)
    u3 = u.reshape(nb * groups, SUBLANES, r)
    row = lax.broadcasted_iota(jnp.int32, (1, SUBLANES, r), 1)
    dist = 1
    while dist < SUBLANES:
        keep = row >= dist
        u3 = jnp.where(keep, a3 * pltpu.roll(u3, dist, 1) + u3, u3)
        a3 = jnp.where(keep, a3 * pltpu.roll(a3, dist, 1), a3)
        dist *= 2
    a4 = a3.reshape(nb, groups, SUBLANES, r)
    u4 = u3.reshape(nb, groups, SUBLANES, r)
    h_prev = h_sc[...]
    hs = []
    for gi in range(groups):
        hs.append(a4[:, gi] * h_prev + u4[:, gi])
        h_prev = hs[-1][:, SUBLANES - 1:, :]
    h = hs[0] if groups == 1 else jnp.concatenate(hs, axis=1)
    h_last = h_prev
    h_sc[...] = h_last
    ht_ref[...] = h_last

    y = (h.reshape(nb * t, r) * _silu(gb)).astype(BF16)
    out = jnp.dot(y, wout_ref[...], preferred_element_type=F32).reshape(nb, t, d)
    y_ref[...] = x + gate * out


def _gate_weights(wa, wx):
    nblk, bw, _ = wa.shape
    ngrp = nblk // GATE_GROUP
    eye = jnp.eye(GATE_GROUP, dtype=wa.dtype)[None, :, None, :, None]

    def block_diag(w):
        w = w.reshape(ngrp, GATE_GROUP, bw, 1, bw) * eye
        return w.reshape(ngrp, GATE_GROUP * bw, GATE_GROUP * bw)

    return jnp.concatenate([block_diag(wa), block_diag(wx)], axis=-1).astype(BF16)


def _lru_layer(x, mod, norm_g, w_in, conv_w, conv_b, wg, ba, bx, lam, w_out, cs0, h0, nb, t):
    b, s, d = x.shape
    r = conv_w.shape[1]
    gw = wg.shape[1]
    row = lambda v: v.reshape(1, -1)
    blk = lambda i, j: (i, j, 0)
    per_b = lambda i, j: (i, 0, 0)
    return pl.pallas_call(
        _lru_body,
        grid=(b // nb, s // t),
        in_specs=[pl.BlockSpec((nb, t, d), blk),
                  pl.BlockSpec((nb, 1, 3 * d), per_b),
                  _const_spec((1, d)),
                  _const_spec((d, 2 * r)),
                  _const_spec((CONV_W, r)),
                  _const_spec((1, r)),
                  _const_spec((r // gw, gw, 2 * gw)),
                  _const_spec((1, r)),
                  _const_spec((1, r)),
                  _const_spec((1, r)),
                  _const_spec((r, d)),
                  pl.BlockSpec((nb, SUBLANES, r), per_b),
                  pl.BlockSpec((nb, 1, r), per_b)],
        out_specs=[pl.BlockSpec((nb, t, d), blk),
                   pl.BlockSpec((nb, SUBLANES, r), per_b),
                   pl.BlockSpec((nb, 1, r), per_b)],
        out_shape=[jax.ShapeDtypeStruct((b, s, d), F32),
                   jax.ShapeDtypeStruct((b, SUBLANES, r), F32),
                   jax.ShapeDtypeStruct((b, 1, r), F32)],
        scratch_shapes=[pltpu.VMEM((nb, SUBLANES, r), F32), pltpu.VMEM((nb, 1, r), F32)],
        compiler_params=_params(2),
        name="lru_layer",
    )(x, mod, row(norm_g), w_in, conv_w, row(conv_b), wg, row(ba), row(bx), row(lam), w_out, cs0, h0)


def _attproj_body(x_ref, mod_ref, g_ref, w_ref, p0_ref, p1_ref, p2_ref, gate_ref, t0_ref, t1_ref, t2_ref,
                  stage_sc, *, att_scale):
    _, t, d = x_ref.shape
    mod = mod_ref[...]
    hn = _adaln_norm(x_ref[...], g_ref[...], mod[..., :d], mod[..., d:2 * d])
    hn = hn.reshape(t, d).astype(BF16)
    tails = (t0_ref, t1_ref, t2_ref)
    qkvs = (p0_ref, p1_ref, p2_ref)
    for c in range(3 * N_GROUPS + 1):
        res = jnp.dot(hn, w_ref[:, c * N_CHUNK:(c + 1) * N_CHUNK], preferred_element_type=F32)
        g, j = divmod(c, 3)
        if g == N_GROUPS:
            gate_ref[...] = res.astype(gate_ref.dtype)
            continue
        if j == 0:
            res = res * att_scale
        else:
            rows = tails[g].shape[0]
            tails[g][:, (j - 1) * N_CHUNK:j * N_CHUNK] = res[t - rows:, :]
        cols = slice(j * N_CHUNK, (j + 1) * N_CHUNK)
        dil = qkvs[g].shape[0]
        if dil == 1:
            qkvs[g][0, :, cols] = res.astype(qkvs[g].dtype)
        else:
            nblk = N_CHUNK // LANES
            for cb in range(nblk):
                stage_sc[cb] = res[:, cb * LANES:(cb + 1) * LANES]
            for r in range(dil):
                picked = [stage_sc[cb, pl.ds(r, t // dil, stride=dil), :] for cb in range(nblk)]
                qkvs[g][r, :, cols] = jnp.concatenate(picked, axis=1).astype(qkvs[g].dtype)


def _att_proj(x, mod, norm_g, w_in, t, dils, tail_lens, p_dtype):
    b, s, d = x.shape
    ncol = w_in.shape[1]
    n_t = s // t
    qkv_shapes = [jax.ShapeDtypeStruct((b, dil, s // dil, 3 * N_CHUNK), p_dtype) for dil in dils]
    qkv_specs = [pl.BlockSpec((None, dil, t // dil, 3 * N_CHUNK), lambda i, j: (i, 0, j, 0)) for dil in dils]
    tail_shapes, tail_specs = [], []
    for g in range(N_GROUPS):
        wlen = tail_lens[g]
        rows = min(wlen, t)
        first = n_t - wlen // rows
        tail_shapes.append(jax.ShapeDtypeStruct((b, wlen, 2 * N_CHUNK), F32))
        tail_specs.append(pl.BlockSpec((None, rows, 2 * N_CHUNK),
                                       lambda i, j, first=first: (i, jnp.maximum(j - first, 0), 0)))
    head_dim = d // N_HEADS
    return pl.pallas_call(
        functools.partial(_attproj_body, att_scale=head_dim ** -0.5),
        grid=(b, n_t),
        in_specs=[pl.BlockSpec((1, t, d), lambda i, j: (i, j, 0)),
                  _mod_spec(mod, 1, t),
                  _const_spec((1, d)),
                  _const_spec((d, ncol))],
        out_specs=qkv_specs + [pl.BlockSpec((None, t, N_CHUNK), lambda i, j: (i, j, 0))] + tail_specs,
        out_shape=qkv_shapes + [jax.ShapeDtypeStruct((b, s, N_CHUNK), p_dtype)] + tail_shapes,
        scratch_shapes=[pltpu.VMEM((N_CHUNK // LANES, t, LANES), F32)],
        compiler_params=_params(2),
        name="att_proj",
    )(x, mod, norm_g.reshape(1, -1), w_in)


def _slope(h):
    return 2.0 ** (-8.0 * (h + 1) / N_HEADS)


def _pattn_body(q_ref, kh_ref, kc_ref, vh_ref, vc_ref, o_ref, lse_ref, k_sc, v_sc, *, dil, n):
    qb = q_ref.shape[0]
    k_sc[:n, :] = kh_ref[...]
    k_sc[n:, :] = kc_ref[...]
    v_sc[:n, :] = vh_ref[...]
    v_sc[n:, :] = vc_ref[...]
    ii = lax.broadcasted_iota(jnp.int32, (n, n), 0)
    jj = lax.broadcasted_iota(jnp.int32, (n, n), 1)
    own = jj <= ii
    eye = jj == ii
    dist_all = (jnp.where(own, ii - jj, ii - jj + n) * dil).astype(F32)
    dist_own = jnp.where(own, dist_all, MASK_DIST)
    lane = lax.broadcasted_iota(jnp.int32, (n, LANES), 1)
    low = lane < (LANES // 2)
    nt = (((1,), (1,)), ((), ()))
    first_chunk = pl.program_id(2) * (qb // n)

    def chunk(sc, carry):
        r0 = pl.multiple_of(sc * n, n)
        has_prev = first_chunk + sc > 0
        dist = jnp.where(has_prev, dist_all, dist_own)
        lse_all = jnp.zeros((n, LANES), F32)
        for p in range(N_HEADS // 2):
            cols = slice(p * LANES, (p + 1) * LANES)
            q2 = q_ref[pl.ds(r0, n), cols]
            k2 = k_sc[pl.ds(r0, 2 * n), cols]
            v2 = v_sc[pl.ds(r0, 2 * n), cols]
            outs = []
            for half in range(2):
                h = 2 * p + half
                qm = jnp.where(low if half == 0 else ~low, q2, jnp.zeros_like(q2))
                s2 = lax.dot_general(qm, k2, nt, preferred_element_type=F32)
                s_prev, s_own = s2[:, :n], s2[:, n:]
                s = jnp.where(own, s_own, s_prev) - _slope(h) * dist
                far = jnp.sum(jnp.where(eye, s_prev, 0.0), axis=-1, keepdims=True)
                far = jnp.where(has_prev, far - _slope(h) * float(n * dil), NEG_INF)
                m = jnp.maximum(jnp.max(s, axis=-1, keepdims=True), far)
                e = jnp.exp(s - m)
                e_far = jnp.exp(far - m)
                l = jnp.sum(e, axis=-1, keepdims=True) + e_far
                e_prev = jnp.where(eye, e_far, jnp.where(own, 0.0, e))
                e_own = jnp.where(own, e, 0.0)
                e2 = jnp.concatenate([e_prev, e_own], axis=1).astype(BF16)
                outs.append(jnp.dot(e2, v2, preferred_element_type=F32) / l)
                lse_all = jnp.where(lane == h, m + jnp.log(l), lse_all)
            o_ref[pl.ds(r0, n), cols] = jnp.where(low, outs[0], outs[1]).astype(o_ref.dtype)
        lse_ref[pl.ds(r0, n), :] = lse_all
        return carry

    lax.fori_loop(0, qb // n, chunk, 0)


def _prompt_attn(qkv, g):
    b, dil, steps, _ = qkv.shape
    n = WINDOWS[g] // dil
    qb = min(steps, ATTN_QUERY_BLOCK)
    cur = lambda col: (lambda i, r, c: (i, r, c, col))
    halo = lambda col: (lambda i, r, c: (i, r, jnp.maximum(c * (qb // n) - 1, 0), col))
    cur_spec = lambda col: pl.BlockSpec((None, None, qb, N_CHUNK), cur(col))
    halo_spec = lambda col: pl.BlockSpec((None, None, n, N_CHUNK), halo(col))
    return pl.pallas_call(
        functools.partial(_pattn_body, dil=dil, n=n),
        grid=(b, dil, steps // qb),
        in_specs=[cur_spec(0), halo_spec(1), cur_spec(1), halo_spec(2), cur_spec(2)],
        out_specs=[pl.BlockSpec((None, None, qb, N_CHUNK), cur(0)),
                   pl.BlockSpec((None, None, qb, LANES), cur(0))],
        out_shape=[jax.ShapeDtypeStruct((b, dil, steps, N_CHUNK), BF16),
                   jax.ShapeDtypeStruct((b, dil, steps, LANES), F32)],
        scratch_shapes=[pltpu.VMEM((qb + n, N_CHUNK), BF16), pltpu.VMEM((qb + n, N_CHUNK), BF16)],
        compiler_params=_params(3),
        name=f"prompt_attn_g{g}",
    )(qkv, qkv, qkv, qkv, qkv)


def _sattn_body(p_ref, cache_ref, o_ref, lse_ref, m_sc, l_sc, acc_sc, *, dil, window):
    t = p_ref.shape[0]
    pc = cache_ref.shape[1]
    hd = N_CHUNK // N_HEADS
    rows = N_HEADS * t
    j = pl.program_id(1)
    buf_len = pc * pl.num_programs(1)
    nt = (((1,), (1,)), ((), ()))

    own_head = (lax.broadcasted_iota(jnp.int32, (rows, N_CHUNK), 0) // t
                == lax.broadcasted_iota(jnp.int32, (rows, N_CHUNK), 1) // hd)
    q = p_ref[:, :N_CHUNK]
    qbd = jnp.where(own_head, jnp.broadcast_to(q[None], (N_HEADS, t, N_CHUNK)).reshape(rows, N_CHUNK), 0.0)
    qbd = qbd.astype(BF16)
    ri = lax.broadcasted_iota(jnp.int32, (rows, 1), 0)
    head = ri // t
    tq = ri - head * t
    slope = jnp.exp((head + 1).astype(F32) * (-8.0 / N_HEADS * math.log(2.0)))

    @pl.when(j == 0)
    def _():
        pad = jnp.zeros((LANES - t, N_CHUNK), F32)
        kn = jnp.concatenate([p_ref[:, N_CHUNK:2 * N_CHUNK], pad], axis=0).astype(BF16)
        vn = jnp.concatenate([p_ref[:, 2 * N_CHUNK:], pad], axis=0).astype(BF16)
        s = lax.dot_general(qbd, kn, nt, preferred_element_type=F32)
        dist = tq - lax.broadcasted_iota(jnp.int32, (rows, LANES), 1)
        ok = (dist >= 0) & ((dist & (dil - 1)) == 0)
        s = jnp.where(ok, s - slope * dist.astype(F32), NEG_INF)
        m = jnp.max(s, axis=-1, keepdims=True)
        e = jnp.exp(s - m)
        m_sc[...] = m
        l_sc[...] = jnp.sum(e, axis=-1, keepdims=True)
        acc_sc[...] = jnp.dot(e.astype(BF16), vn, preferred_element_type=F32)

    kt = cache_ref[:N_CHUNK, :].astype(BF16)
    vt = cache_ref[N_CHUNK:, :].astype(BF16)
    s = jnp.dot(qbd, kt, preferred_element_type=F32)
    pos = j * pc + lax.broadcasted_iota(jnp.int32, (rows, pc), 1)
    dist = buf_len + tq - pos
    ok = ((dist & (dil - 1)) == 0) & (dist <= window)
    s = jnp.where(ok, s - slope * dist.astype(F32), NEG_INF)
    m_old = m_sc[...]
    m_new = jnp.maximum(m_old, jnp.max(s, axis=-1, keepdims=True))
    alpha = jnp.exp(m_old - m_new)
    e = jnp.exp(s - m_new)
    l_new = alpha * l_sc[...] + jnp.sum(e, axis=-1, keepdims=True)
    acc = alpha * acc_sc[...] + lax.dot_general(e.astype(BF16), vt, nt, preferred_element_type=F32)
    m_sc[...] = m_new
    l_sc[...] = l_new
    acc_sc[...] = acc

    @pl.when(j == pl.num_programs(1) - 1)
    def _():
        o = jnp.where(own_head, acc / l_new, 0.0).reshape(N_HEADS, t, N_CHUNK)
        o_ref[...] = jnp.sum(o, axis=0)
        on_lane = (lax.broadcasted_iota(jnp.int32, (rows, LANES), 0) // t
                   == lax.broadcasted_iota(jnp.int32, (rows, LANES), 1))
        lse = jnp.where(on_lane, m_new + jnp.log(l_new), 0.0).reshape(N_HEADS, t, LANES)
        lse_ref[...] = jnp.sum(lse, axis=0)


def _sample_attn(qkv, cache, layer, g):
    b, t, _ = qkv.shape
    natt, _, blen = cache.shape[:3]
    cache_t = jnp.transpose(cache, (0, 1, 3, 4, 5, 2)).reshape(natt, b, 2 * N_CHUNK, blen)
    pc = min(blen, SAMPLE_POS_CHUNK)
    return pl.pallas_call(
        functools.partial(_sattn_body, dil=DILATIONS[g], window=WINDOWS[g]),
        grid=(b, blen // pc),
        in_specs=[pl.BlockSpec((None, t, 3 * N_CHUNK), lambda i, j: (i, 0, 0)),
                  pl.BlockSpec((None, None, 2 * N_CHUNK, pc), lambda i, j: (layer, i, 0, j))],
        out_specs=[pl.BlockSpec((None, t, N_CHUNK), lambda i, j: (i, 0, 0)),
                   pl.BlockSpec((None, t, LANES), lambda i, j: (i, 0, 0))],
        out_shape=[jax.ShapeDtypeStruct((b, t, N_CHUNK), F32),
                   jax.ShapeDtypeStruct((b, t, LANES), F32)],
        scratch_shapes=[pltpu.VMEM((N_HEADS * t, 1), F32), pltpu.VMEM((N_HEADS * t, 1), F32),
                        pltpu.VMEM((N_HEADS * t, N_CHUNK), F32)],
        compiler_params=_params(2),
        name=f"sample_attn_g{g}",
    )(qkv, cache_t)


def _natural_rows(ref, stage_sc):
    dil, n, c = ref.shape
    if dil == 1:
        return ref[0].astype(F32)
    nblk = c // LANES
    for r in range(dil):
        rows = ref[r].astype(F32)
        for cb in range(nblk):
            stage_sc[cb, pl.ds(r, n, stride=dil), :] = rows[:, cb * LANES:(cb + 1) * LANES]
    return jnp.concatenate([stage_sc[cb] for cb in range(nblk)], axis=1)


def _attout_body(x_ref, mod_ref, gate_ref, o0_ref, o1_ref, o2_ref, l0_ref, l1_ref, l2_ref,
                 wout_ref, fg_ref, y_ref, stage_sc, *, final_norm):
    _, t, d = x_ref.shape
    hd = d // N_HEADS
    lses = [_natural_rows(r, stage_sc) for r in (l0_ref, l1_ref, l2_ref)]
    m = jnp.maximum(jnp.maximum(lses[0], lses[1]), lses[2])
    es = [jnp.exp(l - m) for l in lses]
    den = es[0] + es[1] + es[2]
    expand = (lax.broadcasted_iota(jnp.int32, (LANES, d), 0)
              == lax.broadcasted_iota(jnp.int32, (LANES, d), 1) // hd).astype(BF16)
    acc = jnp.zeros((t, d), F32)
    for e, o_ref in zip(es, (o0_ref, o1_ref, o2_ref)):
        w = e / den
        hi = w.astype(BF16)
        lo = (w - hi.astype(F32)).astype(BF16)
        wfull = (jnp.dot(hi, expand, preferred_element_type=F32)
                 + jnp.dot(lo, expand, preferred_element_type=F32))
        acc = acc + wfull * _natural_rows(o_ref, stage_sc)
    y = (acc * _silu(gate_ref[...].astype(F32))).astype(BF16)
    out = jnp.dot(y, wout_ref[...], preferred_element_type=F32)
    xn = x_ref[0] + mod_ref[0][:, 2 * d:] * out
    if final_norm:
        xn = xn * lax.rsqrt(jnp.mean(xn * xn, axis=-1, keepdims=True) + EPS) * fg_ref[...]
    y_ref[0] = xn


def _att_out(x, mod, gate, os_, lses, w_out, final_g, final_norm, t):
    b, s, d = x.shape
    blk = lambda i, j: (i, j, 0)
    res_spec = lambda v: pl.BlockSpec((None, v.shape[1], t // v.shape[1], v.shape[3]), lambda i, j: (i, 0, j, 0))
    return pl.pallas_call(
        functools.partial(_attout_body, final_norm=final_norm),
        grid=(b, s // t),
        in_specs=[pl.BlockSpec((1, t, d), blk),
                  _mod_spec(mod, 1, t),
                  pl.BlockSpec((None, t, N_CHUNK), blk)]
                 + [res_spec(o) for o in os_] + [res_spec(l) for l in lses]
                 + [_const_spec((d, d)), _const_spec((1, d))],
        out_specs=pl.BlockSpec((1, t, d), blk),
        out_shape=jax.ShapeDtypeStruct((b, s, d), F32),
        scratch_shapes=[pltpu.VMEM((N_CHUNK // LANES, t, LANES), F32)],
        compiler_params=_params(2),
        name="att_out",
    )(x, mod, gate, *os_, *lses, w_out, final_g.reshape(1, -1))


def kernel(x_prompt, x_sample, state_conv, state_h, cache_kv_g0, cache_kv_g1, cache_kv_g2, c_prompt, c_sample, norm_g, ada_w, ada_b, final_g, lru_w_in, lru_conv_w, lru_conv_b, lru_wa, lru_ba, lru_wx, lru_bx, lru_lambda, lru_w_out, att_w_in, att_w_out):
    bp, sp, d = x_prompt.shape
    bs, ts, _ = x_sample.shape
    r = lru_conv_w.shape[-1]
    hd = d // N_HEADS
    caches = (cache_kv_g0, cache_kv_g1, cache_kv_g2)

    n_c = bp + bs
    c_all = jnp.concatenate([c_prompt, c_sample, jnp.zeros((-n_c % 16, d), F32)], axis=0)
    mod = _adaln_mod(c_all, ada_w, ada_b)
    mod_p = mod[:, :bp].reshape(DEPTH, bp, 1, 3 * d)
    mod_s = mod[:, bp:n_c].reshape(DEPTH, bs, 1, 3 * d)

    yp, ys = x_prompt, x_sample
    p_conv, p_h, s_conv, s_h = [], [], [], []
    p_kv = [[] for _ in range(N_GROUPS)]
    s_kv = [[] for _ in range(N_GROUPS)]
    n_keep = CONV_W - 1
    for i in range(DEPTH):
        j = i // N_MIXERS
        if i % N_MIXERS == 0:
            wg = _gate_weights(lru_wa[j], lru_wx[j])
            args = (norm_g[i], lru_w_in[j].astype(BF16), lru_conv_w[j], lru_conv_b[j], wg, lru_ba[j],
                    lru_bx[j], lru_lambda[j], lru_w_out[j].astype(BF16))
            zc = jnp.zeros((bp, SUBLANES, r), F32)
            zh = jnp.zeros((bp, 1, r), F32)
            yp, cs, ht = _lru_layer(yp, mod_p[i], *args, zc, zh, 1, PROMPT_TILE)
            p_conv.append(cs[:, SUBLANES - n_keep:])
            p_h.append(ht[:, 0])
            cs0 = jnp.pad(state_conv[j], ((0, 0), (SUBLANES - n_keep, 0), (0, 0)))
            ys, cs, ht = _lru_layer(ys, mod_s[i], *args, cs0, state_h[j][:, None, :], bs, ts)
            s_conv.append(cs[:, SUBLANES - n_keep:])
            s_h.append(ht[:, 0])
        else:
            w_in = att_w_in[j].astype(BF16)
            w_out = att_w_out[j].astype(BF16)
            last = i == DEPTH - 1
            p_tails = [min(w, sp) for w in WINDOWS]
            *qkvs, gate, t0, t1, t2 = _att_proj(yp, mod_p[i], norm_g[i], w_in, PROMPT_TILE, DILATIONS,
                                                p_tails, BF16)
            res = [_prompt_attn(qkvs[g], g) for g in range(N_GROUPS)]
            yp = _att_out(yp, mod_p[i], gate, [o for o, _ in res], [l for _, l in res], w_out, final_g,
                          last, PROMPT_TILE)
            for g, tail in enumerate((t0, t1, t2)):
                p_kv[g].append(tail.reshape(bp, -1, 2, N_HEADS, hd))
            n_s = bs * ts
            flat = lambda v: v.reshape(1, n_s, v.shape[-1])
            mod_rows = flat(jnp.broadcast_to(mod_s[i], (bs, ts, 3 * d)))
            *qkvs, gate, t0, t1, t2 = _att_proj(flat(ys), mod_rows, norm_g[i], w_in, n_s, (1,) * N_GROUPS,
                                                [n_s] * N_GROUPS, F32)
            res = [_sample_attn(qkvs[g].reshape(bs, ts, -1), caches[g], j, g) for g in range(N_GROUPS)]
            nat = lambda v: v.reshape(1, 1, n_s, v.shape[-1])
            ys = _att_out(flat(ys), mod_rows, gate, [nat(o) for o, _ in res], [nat(l) for _, l in res],
                          w_out, final_g, last, n_s).reshape(bs, ts, d)
            for g, tail in enumerate((t0, t1, t2)):
                s_kv[g].append(tail.reshape(bs, ts, 2, N_HEADS, hd))
    return (yp, ys,
            jnp.stack(p_conv), jnp.stack(p_h),
            jnp.stack(p_kv[0]), jnp.stack(p_kv[1]), jnp.stack(p_kv[2]),
            jnp.stack(s_conv), jnp.stack(s_h),
            jnp.stack(s_kv[0]), jnp.stack(s_kv[1]), jnp.stack(s_kv[2]))
```

```python
import functools
import math

import jax
import jax.numpy as jnp
from jax import lax
from jax.experimental import pallas as pl
from jax.experimental.pallas import tpu as pltpu

F32 = jnp.float32
BF16 = jnp.bfloat16

DEPTH = 4
N_MIXERS = 2
N_BLOCKS = 16
CONV_W = 4
LRU_C = 8.0
N_HEADS = 16
WINDOWS = (128, 512, 2048)
DILATIONS = (1, 4, 16)
N_GROUPS = 3
NEG_INF = -1e30
EPS = 1e-6

LANES = 128
SUBLANES = 8
VMEM_LIMIT = 56 * 1024 * 1024
PROMPT_TILE = 256
GATE_GROUP = 4
N_CHUNK = 1024
SAMPLE_POS_CHUNK = 1024
ATTN_QUERY_BLOCK = 512
MASK_DIST = 1e33


def _silu(x):
    return x * jax.nn.sigmoid(x)


def _adaln_norm(x, g, shift, scale):
    y = x * lax.rsqrt(jnp.mean(x * x, axis=-1, keepdims=True) + EPS)
    return (y * g) * (1.0 + scale) + shift


def _const_spec(shape):
    nd = len(shape)
    return pl.BlockSpec(shape, lambda *_: (0,) * nd, pipeline_mode=pl.Buffered(1))


def _params(n_grid):
    return pltpu.CompilerParams(dimension_semantics=("arbitrary",) * n_grid,
                                vmem_limit_bytes=VMEM_LIMIT)


def _mod_spec(mod, nb, t):
    _, mt, d3 = mod.shape
    if mt == 1:
        return pl.BlockSpec((nb, 1, d3), lambda i, j: (i, 0, 0))
    return pl.BlockSpec((nb, t, d3), lambda i, j: (i, j, 0))


def _mod_body(c_ref, w_ref, b_ref, o_ref):
    a = _silu(c_ref[...]).astype(BF16)
    o_ref[...] = jnp.dot(a, w_ref[...].astype(BF16), preferred_element_type=F32) + b_ref[...]


def _adaln_mod(c_all, ada_w, ada_b):
    r, d = c_all.shape
    depth, _, d3 = ada_w.shape
    tn = N_CHUNK
    return pl.pallas_call(
        _mod_body,
        grid=(depth, d3 // tn),
        in_specs=[pl.BlockSpec((r, d), lambda i, n: (0, 0)),
                  pl.BlockSpec((None, d, tn), lambda i, n: (i, 0, n)),
                  pl.BlockSpec((None, 1, tn), lambda i, n: (i, 0, n))],
        out_specs=pl.BlockSpec((None, r, tn), lambda i, n: (i, 0, n)),
        out_shape=jax.ShapeDtypeStruct((depth, r, d3), F32),
        compiler_params=_params(2),
        name="adaln_mod",
    )(c_all, ada_w, ada_b.reshape(depth, 1, d3))


def _lru_body(x_ref, mod_ref, g_ref, win_ref, cw_ref, cb_ref, wg_ref, ba_ref, bx_ref, lam_ref,
              wout_ref, cs0_ref, h0_ref, y_ref, cs_ref, ht_ref, prev_sc, h_sc):
    nb, t, d = x_ref.shape
    r = cw_ref.shape[1]
    gw = wg_ref.shape[1]

    @pl.when(pl.program_id(1) == 0)
    def _():
        prev_sc[...] = cs0_ref[...]
        h_sc[...] = h0_ref[...]

    x = x_ref[...]
    mod = mod_ref[...]
    shift, scale, gate = mod[..., :d], mod[..., d:2 * d], mod[..., 2 * d:]
    hn = _adaln_norm(x, g_ref[...], shift, scale).reshape(nb * t, d).astype(BF16)
    proj = jnp.dot(hn, win_ref[...], preferred_element_type=F32)
    xb = proj[:, :r].reshape(nb, t, r)
    gb = proj[:, r:]

    cat = jnp.concatenate([prev_sc[...], xb], axis=1)
    cw = cw_ref[...]
    xc = cb_ref[...] + xb * cw[CONV_W - 1:CONV_W]
    for k in range(1, CONV_W):
        xc = xc + pltpu.roll(cat, k, 1)[:, SUBLANES:, :] * cw[CONV_W - 1 - k:CONV_W - k]
    new_prev = cat[:, t:, :]
    prev_sc[...] = new_prev
    cs_ref[...] = new_prev

    xc2 = xc.reshape(nb * t, r)
    xcb = xc2.astype(BF16)
    grs, gis = [], []
    for q in range(r // gw):
        p = jnp.dot(xcb[:, q * gw:(q + 1) * gw], wg_ref[q], preferred_element_type=F32)
        grs.append(p[:, :gw])
        gis.append(p[:, gw:])
    rg = jax.nn.sigmoid(jnp.concatenate(grs, axis=1) + ba_ref[...])
    ig = jax.nn.sigmoid(jnp.concatenate(gis, axis=1) + bx_ref[...])
    nlam = -lam_ref[...]
    softplus = jnp.maximum(nlam, 0.0) + jnp.log1p(jnp.exp(-jnp.abs(nlam)))
    log_a = (-LRU_C) * rg * softplus
    a = jnp.exp(log_a)
    em1 = a - 1.0
    em1 = jnp.where(a == 1.0, log_a, jnp.where(a <= 0.5, em1, em1 * log_a / jnp.log(a)))
    u = jnp.sqrt(-em1 * (1.0 + a)) * ig * xc2

    groups = t // SUBLANES
    a3 = a.reshape(nb * groups, SUBLANES, r)
    u3 = u.reshape(nb * groups, SUBLANES, r)
    row = lax.broadcasted_iota(jnp.int32, (1, SUBLANES, r), 1)
    dist = 1
    while dist < SUBLANES:
        keep = row >= dist
        u3 = jnp.where(keep, a3 * pltpu.roll(u3, dist, 1) + u3, u3)
        a3 = jnp.where(keep, a3 * pltpu.roll(a3, dist, 1), a3)
        dist *= 2
    a4 = a3.reshape(nb, groups, SUBLANES, r)
    u4 = u3.reshape(nb, groups, SUBLANES, r)
    h_prev = h_sc[...]
    hs = []
    for gi in range(groups):
        hs.append(a4[:, gi] * h_prev + u4[:, gi])
        h_prev = hs[-1][:, SUBLANES - 1:, :]
    h = hs[0] if groups == 1 else jnp.concatenate(hs, axis=1)
    h_last = h_prev
    h_sc[...] = h_last
    ht_ref[...] = h_last

    y = (h.reshape(nb * t, r) * _silu(gb)).astype(BF16)
    out = jnp.dot(y, wout_ref[...], preferred_element_type=F32).reshape(nb, t, d)
    y_ref[...] = x + gate * out


def _gate_weights(wa, wx):
    nblk, bw, _ = wa.shape
    ngrp = nblk // GATE_GROUP
    eye = jnp.eye(GATE_GROUP, dtype=wa.dtype)[None, :, None, :, None]

    def block_diag(w):
        w = w.reshape(ngrp, GATE_GROUP, bw, 1, bw) * eye
        return w.reshape(ngrp, GATE_GROUP * bw, GATE_GROUP * bw)

    return jnp.concatenate([block_diag(wa), block_diag(wx)], axis=-1).astype(BF16)


def _lru_layer(x, mod, norm_g, w_in, conv_w, conv_b, wg, ba, bx, lam, w_out, cs0, h0, nb, t):
    b, s, d = x.shape
    r = conv_w.shape[1]
    gw = wg.shape[1]
    row = lambda v: v.reshape(1, -1)
    blk = lambda i, j: (i, j, 0)
    per_b = lambda i, j: (i, 0, 0)
    return pl.pallas_call(
        _lru_body,
        grid=(b // nb, s // t),
        in_specs=[pl.BlockSpec((nb, t, d), blk),
                  pl.BlockSpec((nb, 1, 3 * d), per_b),
                  _const_spec((1, d)),
                  _const_spec((d, 2 * r)),
                  _const_spec((CONV_W, r)),
                  _const_spec((1, r)),
                  _const_spec((r // gw, gw, 2 * gw)),
                  _const_spec((1, r)),
                  _const_spec((1, r)),
                  _const_spec((1, r)),
                  _const_spec((r, d)),
                  pl.BlockSpec((nb, SUBLANES, r), per_b),
                  pl.BlockSpec((nb, 1, r), per_b)],
        out_specs=[pl.BlockSpec((nb, t, d), blk),
                   pl.BlockSpec((nb, SUBLANES, r), per_b),
                   pl.BlockSpec((nb, 1, r), per_b)],
        out_shape=[jax.ShapeDtypeStruct((b, s, d), F32),
                   jax.ShapeDtypeStruct((b, SUBLANES, r), F32),
                   jax.ShapeDtypeStruct((b, 1, r), F32)],
        scratch_shapes=[pltpu.VMEM((nb, SUBLANES, r), F32), pltpu.VMEM((nb, 1, r), F32)],
        compiler_params=_params(2),
        name="lru_layer",
    )(x, mod, row(norm_g), w_in, conv_w, row(conv_b), wg, row(ba), row(bx), row(lam), w_out, cs0, h0)


def _attproj_body(x_ref, mod_ref, g_ref, w_ref, p0_ref, p1_ref, p2_ref, gate_ref, t0_ref, t1_ref, t2_ref,
                  stage_sc, *, att_scale):
    _, t, d = x_ref.shape
    mod = mod_ref[...]
    hn = _adaln_norm(x_ref[...], g_ref[...], mod[..., :d], mod[..., d:2 * d])
    hn = hn.reshape(t, d).astype(BF16)
    tails = (t0_ref, t1_ref, t2_ref)
    qkvs = (p0_ref, p1_ref, p2_ref)
    for c in range(3 * N_GROUPS + 1):
        res = jnp.dot(hn, w_ref[:, c * N_CHUNK:(c + 1) * N_CHUNK], preferred_element_type=F32)
        g, j = divmod(c, 3)
        if g == N_GROUPS:
            gate_ref[...] = res.astype(gate_ref.dtype)
            continue
        if j == 0:
            res = res * att_scale
        else:
            rows = tails[g].shape[0]
            tails[g][:, (j - 1) * N_CHUNK:j * N_CHUNK] = res[t - rows:, :]
        cols = slice(j * N_CHUNK, (j + 1) * N_CHUNK)
        dil = qkvs[g].shape[0]
        if dil == 1:
            qkvs[g][0, :, cols] = res.astype(qkvs[g].dtype)
        else:
            nblk = N_CHUNK // LANES
            for cb in range(nblk):
                stage_sc[cb] = res[:, cb * LANES:(cb + 1) * LANES]
            for r in range(dil):
                picked = [stage_sc[cb, pl.ds(r, t // dil, stride=dil), :] for cb in range(nblk)]
                qkvs[g][r, :, cols] = jnp.concatenate(picked, axis=1).astype(qkvs[g].dtype)


def _att_proj(x, mod, norm_g, w_in, t, dils, tail_lens, p_dtype):
    b, s, d = x.shape
    ncol = w_in.shape[1]
    n_t = s // t
    qkv_shapes = [jax.ShapeDtypeStruct((b, dil, s // dil, 3 * N_CHUNK), p_dtype) for dil in dils]
    qkv_specs = [pl.BlockSpec((None, dil, t // dil, 3 * N_CHUNK), lambda i, j: (i, 0, j, 0)) for dil in dils]
    tail_shapes, tail_specs = [], []
    for g in range(N_GROUPS):
        wlen = tail_lens[g]
        rows = min(wlen, t)
        first = n_t - wlen // rows
        tail_shapes.append(jax.ShapeDtypeStruct((b, wlen, 2 * N_CHUNK), F32))
        tail_specs.append(pl.BlockSpec((None, rows, 2 * N_CHUNK),
                                       lambda i, j, first=first: (i, jnp.maximum(j - first, 0), 0)))
    head_dim = d // N_HEADS
    return pl.pallas_call(
        functools.partial(_attproj_body, att_scale=head_dim ** -0.5),
        grid=(b, n_t),
        in_specs=[pl.BlockSpec((1, t, d), lambda i, j: (i, j, 0)),
                  _mod_spec(mod, 1, t),
                  _const_spec((1, d)),
                  _const_spec((d, ncol))],
        out_specs=qkv_specs + [pl.BlockSpec((None, t, N_CHUNK), lambda i, j: (i, j, 0))] + tail_specs,
        out_shape=qkv_shapes + [jax.ShapeDtypeStruct((b, s, N_CHUNK), p_dtype)] + tail_shapes,
        scratch_shapes=[pltpu.VMEM((N_CHUNK // LANES, t, LANES), F32)],
        compiler_params=_params(2),
        name="att_proj",
    )(x, mod, norm_g.reshape(1, -1), w_in)


def _slope(h):
    return 2.0 ** (-8.0 * (h + 1) / N_HEADS)


def _pattn_body(q_ref, kh_ref, kc_ref, vh_ref, vc_ref, o_ref, m_ref, l_ref, k_sc, v_sc, pen_sc, *, dil, n):
    qb = q_ref.shape[0]

    @pl.when((pl.program_id(0) == 0) & (pl.program_id(1) == 0) & (pl.program_id(2) == 0))
    def _():
        ii = lax.broadcasted_iota(jnp.int32, (n, 2 * n), 0)
        jj = lax.broadcasted_iota(jnp.int32, (n, 2 * n), 1)
        diff = ii - jj + n
        dist = jnp.where((diff >= 0) & (diff <= n), (diff * dil).astype(F32), MASK_DIST)
        dist_first = jnp.where(jj >= n, dist, MASK_DIST)
        for h in range(N_HEADS):
            pen_sc[0, h] = _slope(h) * dist
            pen_sc[1, h] = _slope(h) * dist_first

    k_sc[:n, :] = kh_ref[...]
    k_sc[n:, :] = kc_ref[...]
    v_sc[:n, :] = vh_ref[...]
    v_sc[n:, :] = vc_ref[...]
    lane = lax.broadcasted_iota(jnp.int32, (n, LANES), 1)
    low = lane < (LANES // 2)
    ones = jnp.ones((2 * n, LANES), BF16)
    nt = (((1,), (1,)), ((), ()))
    for sc in range(qb // n):
        r0 = sc * n
        table = jnp.where(pl.program_id(2) > 0, 0, 1) if sc == 0 else 0
        m_all = jnp.zeros((n, LANES), F32)
        l_all = jnp.ones((n, LANES), F32)
        for p in range(N_HEADS // 2):
            cols = slice(p * LANES, (p + 1) * LANES)
            q2 = q_ref[r0:r0 + n, cols]
            k2 = k_sc[r0:r0 + 2 * n, cols]
            v2 = jnp.concatenate([v_sc[r0:r0 + 2 * n, cols], ones], axis=1)
            outs = []
            for half in range(2):
                h = 2 * p + half
                qm = jnp.where(low if half == 0 else ~low, q2, jnp.zeros_like(q2))
                s = lax.dot_general(qm, k2, nt, preferred_element_type=F32) - pen_sc[table, h]
                m = jnp.max(s, axis=-1, keepdims=True)
                e = jnp.exp(s - m).astype(BF16)
                ov = jnp.dot(e, v2, preferred_element_type=F32)
                outs.append(ov[:, :LANES])
                m_all = jnp.where(lane == h, m, m_all)
                l_all = jnp.where(lane == h, ov[:, LANES:], l_all)
            o_ref[r0:r0 + n, cols] = jnp.where(low, outs[0], outs[1]).astype(o_ref.dtype)
        m_ref[r0:r0 + n, :] = m_all
        l_ref[r0:r0 + n, :] = l_all


def _prompt_attn(qkv, g):
    b, dil, steps, _ = qkv.shape
    n = WINDOWS[g] // dil
    qb = min(steps, ATTN_QUERY_BLOCK)
    cur = lambda col: (lambda i, r, c: (i, r, c, col))
    halo = lambda col: (lambda i, r, c: (i, r, jnp.maximum(c * (qb // n) - 1, 0), col))
    cur_spec = lambda col: pl.BlockSpec((None, None, qb, N_CHUNK), cur(col))
    halo_spec = lambda col: pl.BlockSpec((None, None, n, N_CHUNK), halo(col))
    return pl.pallas_call(
        functools.partial(_pattn_body, dil=dil, n=n),
        grid=(b, dil, steps // qb),
        in_specs=[cur_spec(0), halo_spec(1), cur_spec(1), halo_spec(2), cur_spec(2)],
        out_specs=[pl.BlockSpec((None, None, qb, N_CHUNK), cur(0)),
                   pl.BlockSpec((None, None, qb, LANES), cur(0)),
                   pl.BlockSpec((None, None, qb, LANES), cur(0))],
        out_shape=[jax.ShapeDtypeStruct((b, dil, steps, N_CHUNK), BF16),
                   jax.ShapeDtypeStruct((b, dil, steps, LANES), F32),
                   jax.ShapeDtypeStruct((b, dil, steps, LANES), F32)],
        scratch_shapes=[pltpu.VMEM((qb + n, N_CHUNK), BF16), pltpu.VMEM((qb + n, N_CHUNK), BF16),
                        pltpu.VMEM((2, N_HEADS, n, 2 * n), F32)],
        compiler_params=_params(3),
        name=f"prompt_attn_g{g}",
    )(qkv, qkv, qkv, qkv, qkv)


def _sattn_body(p_ref, cache_ref, o_ref, m_ref, l_ref, m_sc, l_sc, acc_sc, *, dil, window):
    t = p_ref.shape[0]
    pc = cache_ref.shape[1]
    hd = N_CHUNK // N_HEADS
    rows = N_HEADS * t
    j = pl.program_id(1)
    buf_len = pc * pl.num_programs(1)
    nt = (((1,), (1,)), ((), ()))

    own_head = (lax.broadcasted_iota(jnp.int32, (rows, N_CHUNK), 0) // t
                == lax.broadcasted_iota(jnp.int32, (rows, N_CHUNK), 1) // hd)
    q = p_ref[:, :N_CHUNK]
    qbd = jnp.where(own_head, jnp.broadcast_to(q[None], (N_HEADS, t, N_CHUNK)).reshape(rows, N_CHUNK), 0.0)
    qbd = qbd.astype(BF16)
    ri = lax.broadcasted_iota(jnp.int32, (rows, 1), 0)
    head = ri // t
    tq = ri - head * t
    slope = jnp.exp((head + 1).astype(F32) * (-8.0 / N_HEADS * math.log(2.0)))

    @pl.when(j == 0)
    def _():
        pad = jnp.zeros((LANES - t, N_CHUNK), F32)
        kn = jnp.concatenate([p_ref[:, N_CHUNK:2 * N_CHUNK], pad], axis=0).astype(BF16)
        vn = jnp.concatenate([p_ref[:, 2 * N_CHUNK:], pad], axis=0).astype(BF16)
        s = lax.dot_general(qbd, kn, nt, preferred_element_type=F32)
        dist = tq - lax.broadcasted_iota(jnp.int32, (rows, LANES), 1)
        ok = (dist >= 0) & ((dist & (dil - 1)) == 0)
        s = jnp.where(ok, s - slope * dist.astype(F32), NEG_INF)
        m = jnp.max(s, axis=-1, keepdims=True)
        e = jnp.exp(s - m)
        m_sc[...] = m
        l_sc[...] = jnp.sum(e, axis=-1, keepdims=True)
        acc_sc[...] = jnp.dot(e.astype(BF16), vn, preferred_element_type=F32)

    kt = cache_ref[:N_CHUNK, :].astype(BF16)
    vt = cache_ref[N_CHUNK:, :].astype(BF16)
    s = jnp.dot(qbd, kt, preferred_element_type=F32)
    pos = j * pc + lax.broadcasted_iota(jnp.int32, (rows, pc), 1)
    dist = buf_len + tq - pos
    ok = ((dist & (dil - 1)) == 0) & (dist <= window)
    s = jnp.where(ok, s - slope * dist.astype(F32), NEG_INF)
    m_old = m_sc[...]
    m_new = jnp.maximum(m_old, jnp.max(s, axis=-1, keepdims=True))
    alpha = jnp.exp(m_old - m_new)
    e = jnp.exp(s - m_new)
    l_new = alpha * l_sc[...] + jnp.sum(e, axis=-1, keepdims=True)
    acc = alpha * acc_sc[...] + lax.dot_general(e.astype(BF16), vt, nt, preferred_element_type=F32)
    m_sc[...] = m_new
    l_sc[...] = l_new
    acc_sc[...] = acc

    @pl.when(j == pl.num_programs(1) - 1)
    def _():
        o = jnp.where(own_head, acc, 0.0).reshape(N_HEADS, t, N_CHUNK)
        o_ref[...] = jnp.sum(o, axis=0)
        on_lane = (lax.broadcasted_iota(jnp.int32, (rows, LANES), 0) // t
                   == lax.broadcasted_iota(jnp.int32, (rows, LANES), 1))
        unused = lax.broadcasted_iota(jnp.int32, (t, LANES), 1) >= N_HEADS
        m_ref[...] = jnp.sum(jnp.where(on_lane, m_new, 0.0).reshape(N_HEADS, t, LANES), axis=0)
        l_ref[...] = jnp.where(unused, 1.0, jnp.sum(jnp.where(on_lane, l_new, 0.0).reshape(N_HEADS, t, LANES), axis=0))


def _sample_attn(qkv, cache, layer, g):
    b, t, _ = qkv.shape
    natt, _, blen = cache.shape[:3]
    cache_t = jnp.transpose(cache, (0, 1, 3, 4, 5, 2)).reshape(natt, b, 2 * N_CHUNK, blen)
    pc = min(blen, SAMPLE_POS_CHUNK)
    return pl.pallas_call(
        functools.partial(_sattn_body, dil=DILATIONS[g], window=WINDOWS[g]),
        grid=(b, blen // pc),
        in_specs=[pl.BlockSpec((None, t, 3 * N_CHUNK), lambda i, j: (i, 0, 0)),
                  pl.BlockSpec((None, None, 2 * N_CHUNK, pc), lambda i, j: (layer, i, 0, j))],
        out_specs=[pl.BlockSpec((None, t, N_CHUNK), lambda i, j: (i, 0, 0)),
                   pl.BlockSpec((None, t, LANES), lambda i, j: (i, 0, 0)),
                   pl.BlockSpec((None, t, LANES), lambda i, j: (i, 0, 0))],
        out_shape=[jax.ShapeDtypeStruct((b, t, N_CHUNK), F32),
                   jax.ShapeDtypeStruct((b, t, LANES), F32),
                   jax.ShapeDtypeStruct((b, t, LANES), F32)],
        scratch_shapes=[pltpu.VMEM((N_HEADS * t, 1), F32), pltpu.VMEM((N_HEADS * t, 1), F32),
                        pltpu.VMEM((N_HEADS * t, N_CHUNK), F32)],
        compiler_params=_params(2),
        name=f"sample_attn_g{g}",
    )(qkv, cache_t)


def _natural_rows(ref, stage_sc):
    dil, n, c = ref.shape
    if dil == 1:
        return ref[0].astype(F32)
    nblk = c // LANES
    for r in range(dil):
        rows = ref[r].astype(F32)
        for cb in range(nblk):
            stage_sc[cb, pl.ds(r, n, stride=dil), :] = rows[:, cb * LANES:(cb + 1) * LANES]
    return jnp.concatenate([stage_sc[cb] for cb in range(nblk)], axis=1)


def _attout_body(x_ref, mod_ref, gate_ref, o0_ref, o1_ref, o2_ref, m0_ref, m1_ref, m2_ref,
                 l0_ref, l1_ref, l2_ref, wout_ref, fg_ref, y_ref, stage_sc, *, final_norm):
    _, t, d = x_ref.shape
    hd = d // N_HEADS
    ms = [_natural_rows(r, stage_sc) for r in (m0_ref, m1_ref, m2_ref)]
    ls = [_natural_rows(r, stage_sc) for r in (l0_ref, l1_ref, l2_ref)]
    m = jnp.maximum(jnp.maximum(ms[0], ms[1]), ms[2])
    es = [jnp.exp(mg - m) for mg in ms]
    den = es[0] * ls[0] + es[1] * ls[1] + es[2] * ls[2]
    expand = (lax.broadcasted_iota(jnp.int32, (LANES, d), 0)
              == lax.broadcasted_iota(jnp.int32, (LANES, d), 1) // hd).astype(BF16)
    acc = jnp.zeros((t, d), F32)
    for e, o_ref in zip(es, (o0_ref, o1_ref, o2_ref)):
        w = e / den
        hi = w.astype(BF16)
        lo = (w - hi.astype(F32)).astype(BF16)
        wfull = (jnp.dot(hi, expand, preferred_element_type=F32)
                 + jnp.dot(lo, expand, preferred_element_type=F32))
        acc = acc + wfull * _natural_rows(o_ref, stage_sc)
    y = (acc * _silu(gate_ref[...].astype(F32))).astype(BF16)
    out = jnp.dot(y, wout_ref[...], preferred_element_type=F32)
    xn = x_ref[0] + mod_ref[0][:, 2 * d:] * out
    if final_norm:
        xn = xn * lax.rsqrt(jnp.mean(xn * xn, axis=-1, keepdims=True) + EPS) * fg_ref[...]
    y_ref[0] = xn


def _att_out(x, mod, gate, os_, ms, ls, w_out, final_g, final_norm, t):
    b, s, d = x.shape
    blk = lambda i, j: (i, j, 0)
    res_spec = lambda v: pl.BlockSpec((None, v.shape[1], t // v.shape[1], v.shape[3]), lambda i, j: (i, 0, j, 0))
    return pl.pallas_call(
        functools.partial(_attout_body, final_norm=final_norm),
        grid=(b, s // t),
        in_specs=[pl.BlockSpec((1, t, d), blk),
                  _mod_spec(mod, 1, t),
                  pl.BlockSpec((None, t, N_CHUNK), blk)]
                 + [res_spec(v) for v in (*os_, *ms, *ls)]
                 + [_const_spec((d, d)), _const_spec((1, d))],
        out_specs=pl.BlockSpec((1, t, d), blk),
        out_shape=jax.ShapeDtypeStruct((b, s, d), F32),
        scratch_shapes=[pltpu.VMEM((N_CHUNK // LANES, t, LANES), F32)],
        compiler_params=_params(2),
        name="att_out",
    )(x, mod, gate, *os_, *ms, *ls, w_out, final_g.reshape(1, -1))


def kernel(x_prompt, x_sample, state_conv, state_h, cache_kv_g0, cache_kv_g1, cache_kv_g2, c_prompt, c_sample, norm_g, ada_w, ada_b, final_g, lru_w_in, lru_conv_w, lru_conv_b, lru_wa, lru_ba, lru_wx, lru_bx, lru_lambda, lru_w_out, att_w_in, att_w_out):
    bp, sp, d = x_prompt.shape
    bs, ts, _ = x_sample.shape
    r = lru_conv_w.shape[-1]
    hd = d // N_HEADS
    caches = (cache_kv_g0, cache_kv_g1, cache_kv_g2)

    n_c = bp + bs
    c_all = jnp.concatenate([c_prompt, c_sample, jnp.zeros((-n_c % 16, d), F32)], axis=0)
    mod = _adaln_mod(c_all, ada_w, ada_b)
    mod_p = mod[:, :bp].reshape(DEPTH, bp, 1, 3 * d)
    mod_s = mod[:, bp:n_c].reshape(DEPTH, bs, 1, 3 * d)

    yp, ys = x_prompt, x_sample
    p_conv, p_h, s_conv, s_h = [], [], [], []
    p_kv = [[] for _ in range(N_GROUPS)]
    s_kv = [[] for _ in range(N_GROUPS)]
    n_keep = CONV_W - 1
    for i in range(DEPTH):
        j = i // N_MIXERS
        if i % N_MIXERS == 0:
            wg = _gate_weights(lru_wa[j], lru_wx[j])
            args = (norm_g[i], lru_w_in[j].astype(BF16), lru_conv_w[j], lru_conv_b[j], wg, lru_ba[j],
                    lru_bx[j], lru_lambda[j], lru_w_out[j].astype(BF16))
            zc = jnp.zeros((bp, SUBLANES, r), F32)
            zh = jnp.zeros((bp, 1, r), F32)
            yp, cs, ht = _lru_layer(yp, mod_p[i], *args, zc, zh, 1, PROMPT_TILE)
            p_conv.append(cs[:, SUBLANES - n_keep:])
            p_h.append(ht[:, 0])
            cs0 = jnp.pad(state_conv[j], ((0, 0), (SUBLANES - n_keep, 0), (0, 0)))
            ys, cs, ht = _lru_layer(ys, mod_s[i], *args, cs0, state_h[j][:, None, :], bs, ts)
            s_conv.append(cs[:, SUBLANES - n_keep:])
            s_h.append(ht[:, 0])
        else:
            w_in = att_w_in[j].astype(BF16)
            w_out = att_w_out[j].astype(BF16)
            last = i == DEPTH - 1
            p_tails = [min(w, sp) for w in WINDOWS]
            *qkvs, gate, t0, t1, t2 = _att_proj(yp, mod_p[i], norm_g[i], w_in, PROMPT_TILE, DILATIONS,
                                                p_tails, BF16)
            res = [_prompt_attn(qkvs[g], g) for g in range(N_GROUPS)]
            yp = _att_out(yp, mod_p[i], gate, *zip(*res), w_out, final_g, last, PROMPT_TILE)
            for g, tail in enumerate((t0, t1, t2)):
                p_kv[g].append(tail.reshape(bp, -1, 2, N_HEADS, hd))
            n_s = bs * ts
            flat = lambda v: v.reshape(1, n_s, v.shape[-1])
            mod_rows = flat(jnp.broadcast_to(mod_s[i], (bs, ts, 3 * d)))
            *qkvs, gate, t0, t1, t2 = _att_proj(flat(ys), mod_rows, norm_g[i], w_in, n_s, (1,) * N_GROUPS,
                                                [n_s] * N_GROUPS, F32)
            res = [_sample_attn(qkvs[g].reshape(bs, ts, -1), caches[g], j, g) for g in range(N_GROUPS)]
            nat = lambda v: v.reshape(1, 1, n_s, v.shape[-1])
            res = [[nat(v) for v in group] for group in res]
            ys = _att_out(flat(ys), mod_rows, gate, *zip(*res), w_out, final_g, last, n_s).reshape(bs, ts, d)
            for g, tail in enumerate((t0, t1, t2)):
                s_kv[g].append(tail.reshape(bs, ts, 2, N_HEADS, hd))
    return (yp, ys,
            jnp.stack(p_conv), jnp.stack(p_h),
            jnp.stack(p_kv[0]), jnp.stack(p_kv[1]), jnp.stack(p_kv[2]),
            jnp.stack(s_conv), jnp.stack(s_h),
            jnp.stack(s_kv[0]), jnp.stack(s_kv[1]), jnp.stack(s_kv[2]))
```

```python
import functools
import math

import jax
import jax.numpy as jnp
from jax import lax
from jax.experimental import pallas as pl
from jax.experimental.pallas import tpu as pltpu

F32 = jnp.float32
BF16 = jnp.bfloat16

DEPTH = 4
N_MIXERS = 2
N_BLOCKS = 16
CONV_W = 4
LRU_C = 8.0
N_HEADS = 16
WINDOWS = (128, 512, 2048)
DILATIONS = (1, 4, 16)
N_GROUPS = 3
NEG_INF = -1e30
EPS = 1e-6

LANES = 128
SUBLANES = 8
VMEM_LIMIT = 56 * 1024 * 1024
PROMPT_TILE = 256
GATE_GROUP = 4
N_CHUNK = 1024
SAMPLE_POS_CHUNK = 2048
SAMPLE_CACHE_BLOCK_BYTES = 8 * 1024 * 1024
ATTN_QUERY_BLOCK = 512
MASK_DIST = 1e33


def _silu(x):
    return x * jax.nn.sigmoid(x)


def _adaln_norm(x, g, shift, scale):
    y = x * lax.rsqrt(jnp.mean(x * x, axis=-1, keepdims=True) + EPS)
    return (y * g) * (1.0 + scale) + shift


def _const_spec(shape):
    nd = len(shape)
    return pl.BlockSpec(shape, lambda *_: (0,) * nd, pipeline_mode=pl.Buffered(1))


def _layer_spec(stacked, layer):
    shape = stacked.shape[1:]
    nd = len(shape)
    return pl.BlockSpec((None,) + shape, lambda *_: (layer,) + (0,) * nd, pipeline_mode=pl.Buffered(1))


def _params(n_grid):
    return pltpu.CompilerParams(dimension_semantics=("arbitrary",) * n_grid,
                                vmem_limit_bytes=VMEM_LIMIT)


def _mod_spec(mod, nb, t):
    _, mt, d3 = mod.shape
    if mt == 1:
        return pl.BlockSpec((nb, 1, d3), lambda i, j: (i, 0, 0))
    return pl.BlockSpec((nb, t, d3), lambda i, j: (i, j, 0))


def _mod_body(c_ref, w_ref, b_ref, o_ref):
    a = _silu(c_ref[...]).astype(BF16)
    o_ref[...] = jnp.dot(a, w_ref[...].astype(BF16), preferred_element_type=F32) + b_ref[...]


def _adaln_mod(c_all, ada_w, ada_b):
    r, d = c_all.shape
    depth, _, d3 = ada_w.shape
    tn = N_CHUNK
    return pl.pallas_call(
        _mod_body,
        grid=(depth, d3 // tn),
        in_specs=[pl.BlockSpec((r, d), lambda i, n: (0, 0)),
                  pl.BlockSpec((None, d, tn), lambda i, n: (i, 0, n)),
                  pl.BlockSpec((None, 1, tn), lambda i, n: (i, 0, n))],
        out_specs=pl.BlockSpec((None, r, tn), lambda i, n: (i, 0, n)),
        out_shape=jax.ShapeDtypeStruct((depth, r, d3), F32),
        compiler_params=_params(2),
        name="adaln_mod",
    )(c_all, ada_w, ada_b.reshape(depth, 1, d3))


def _lru_body(x_ref, mod_ref, g_ref, win_ref, cw_ref, cb_ref, wg_ref, ba_ref, bx_ref, lam_ref,
              wout_ref, cs0_ref, h0_ref, y_ref, cs_ref, ht_ref, prev_sc, h_sc):
    nb, t, d = x_ref.shape
    taps, r = cw_ref.shape
    assert taps == 4, "the conv below pairs its taps two by two"
    gw = wg_ref.shape[1]

    @pl.when(pl.program_id(1) == 0)
    def _():
        prev_sc[...] = cs0_ref[...]
        h_sc[...] = h0_ref[...]

    x = x_ref[...]
    mod = mod_ref[...]
    shift, scale, gate = mod[..., :d], mod[..., d:2 * d], mod[..., 2 * d:]
    hn = _adaln_norm(x, g_ref[...], shift, scale).reshape(nb * t, d).astype(BF16)
    proj = jnp.dot(hn, win_ref[...], preferred_element_type=F32)
    xb = proj[:, :r].reshape(nb, t, r)
    gb = proj[:, r:]

    cat = jnp.concatenate([prev_sc[...], xb], axis=1)
    cw = cw_ref[...]
    back1 = pltpu.roll(cat, 1, 1)
    older = pltpu.roll(cat * cw[1:2] + back1 * cw[0:1], 2, 1)
    xc = cb_ref[...] + xb * cw[3:4] + back1[:, SUBLANES:, :] * cw[2:3] + older[:, SUBLANES:, :]
    new_prev = cat[:, t:, :]
    prev_sc[...] = new_prev
    cs_ref[...] = new_prev

    xc2 = xc.reshape(nb * t, r)
    xcb = xc2.astype(BF16)
    grs, gis = [], []
    for q in range(r // gw):
        p = jnp.dot(xcb[:, q * gw:(q + 1) * gw], wg_ref[q], preferred_element_type=F32)
        grs.append(p[:, :gw])
        gis.append(p[:, gw:])
    rg = jax.nn.sigmoid(jnp.concatenate(grs, axis=1) + ba_ref[...])
    ig = jax.nn.sigmoid(jnp.concatenate(gis, axis=1) + bx_ref[...])
    nlam = -lam_ref[...]
    softplus = jnp.maximum(nlam, 0.0) + jnp.log1p(jnp.exp(-jnp.abs(nlam)))
    z = LRU_C * rg * softplus
    a = jnp.exp(-z)
    u = jnp.sqrt(jnp.tanh(z) * (1.0 + a * a)) * ig * xc2

    groups = t // SUBLANES
    a3 = a.reshape(nb * groups, SUBLANES, r)
    u3 = u.reshape(nb * groups, SUBLANES, r)
    row = lax.broadcasted_iota(jnp.int32, (1, SUBLANES, r), 1)
    dist = 1
    while dist < SUBLANES:
        keep = row >= dist
        u3 = jnp.where(keep, a3 * pltpu.roll(u3, dist, 1) + u3, u3)
        a3 = jnp.where(keep, a3 * pltpu.roll(a3, dist, 1), a3)
        dist *= 2
    a4 = a3.reshape(nb, groups, SUBLANES, r)
    u4 = u3.reshape(nb, groups, SUBLANES, r)
    h_prev = h_sc[...]
    hs = []
    for gi in range(groups):
        hs.append(a4[:, gi] * h_prev + u4[:, gi])
        h_prev = hs[-1][:, SUBLANES - 1:, :]
    h = hs[0] if groups == 1 else jnp.concatenate(hs, axis=1)
    h_last = h_prev
    h_sc[...] = h_last
    ht_ref[...] = h_last

    y = (h.reshape(nb * t, r) * _silu(gb)).astype(BF16)
    out = jnp.dot(y, wout_ref[...], preferred_element_type=F32).reshape(nb, t, d)
    y_ref[...] = x + gate * out


def _gate_weights(wa, wx):
    nblk, bw, _ = wa.shape
    ngrp = nblk // GATE_GROUP
    eye = jnp.eye(GATE_GROUP, dtype=wa.dtype)[None, :, None, :, None]

    def block_diag(w):
        w = w.reshape(ngrp, GATE_GROUP, bw, 1, bw) * eye
        return w.reshape(ngrp, GATE_GROUP * bw, GATE_GROUP * bw)

    return jnp.concatenate([block_diag(wa), block_diag(wx)], axis=-1).astype(BF16)


def _lru_layer(x, mod, layer, norm_g, w_in, conv_w, conv_b, wg, ba, bx, lam, w_out, cs0, h0, nb, t):
    b, s, d = x.shape
    r = conv_w.shape[1]
    gw = wg.shape[1]
    row = lambda v: v.reshape(1, -1)
    blk = lambda i, j: (i, j, 0)
    per_b = lambda i, j: (i, 0, 0)
    return pl.pallas_call(
        _lru_body,
        grid=(b // nb, s // t),
        in_specs=[pl.BlockSpec((nb, t, d), blk),
                  pl.BlockSpec((nb, 1, 3 * d), per_b),
                  _const_spec((1, d)),
                  _layer_spec(w_in, layer),
                  _const_spec((CONV_W, r)),
                  _const_spec((1, r)),
                  _const_spec((r // gw, gw, 2 * gw)),
                  _const_spec((1, r)),
                  _const_spec((1, r)),
                  _const_spec((1, r)),
                  _layer_spec(w_out, layer),
                  pl.BlockSpec((nb, SUBLANES, r), per_b),
                  pl.BlockSpec((nb, 1, r), per_b)],
        out_specs=[pl.BlockSpec((nb, t, d), blk),
                   pl.BlockSpec((nb, SUBLANES, r), per_b),
                   pl.BlockSpec((nb, 1, r), per_b)],
        out_shape=[jax.ShapeDtypeStruct((b, s, d), F32),
                   jax.ShapeDtypeStruct((b, SUBLANES, r), F32),
                   jax.ShapeDtypeStruct((b, 1, r), F32)],
        scratch_shapes=[pltpu.VMEM((nb, SUBLANES, r), F32), pltpu.VMEM((nb, 1, r), F32)],
        compiler_params=_params(2),
        name="lru_layer",
    )(x, mod, row(norm_g), w_in, conv_w, row(conv_b), wg, row(ba), row(bx), row(lam), w_out, cs0, h0)


def _attproj_body(x_ref, mod_ref, g_ref, w_ref, p0_ref, p1_ref, p2_ref, gate_ref, t0_ref, t1_ref, t2_ref,
                  stage_sc, *, att_scale):
    _, t, d = x_ref.shape
    mod = mod_ref[...]
    hn = _adaln_norm(x_ref[...], g_ref[...], mod[..., :d], mod[..., d:2 * d])
    hn = hn.reshape(t, d).astype(BF16)
    tails = (t0_ref, t1_ref, t2_ref)
    qkvs = (p0_ref, p1_ref, p2_ref)
    for c in range(3 * N_GROUPS + 1):
        res = jnp.dot(hn, w_ref[:, c * N_CHUNK:(c + 1) * N_CHUNK], preferred_element_type=F32)
        g, j = divmod(c, 3)
        if g == N_GROUPS:
            gate_ref[...] = res.astype(gate_ref.dtype)
            continue
        if j == 0:
            res = res * att_scale
        else:
            rows = tails[g].shape[0]
            tails[g][:, (j - 1) * N_CHUNK:j * N_CHUNK] = res[t - rows:, :]
        cols = slice(j * N_CHUNK, (j + 1) * N_CHUNK)
        dil = qkvs[g].shape[0]
        if dil == 1:
            qkvs[g][0, :, cols] = res.astype(qkvs[g].dtype)
        else:
            nblk = N_CHUNK // LANES
            for cb in range(nblk):
                stage_sc[cb] = res[:, cb * LANES:(cb + 1) * LANES]
            for r in range(dil):
                picked = [stage_sc[cb, pl.ds(r, t // dil, stride=dil), :] for cb in range(nblk)]
                qkvs[g][r, :, cols] = jnp.concatenate(picked, axis=1).astype(qkvs[g].dtype)


def _att_proj(x, mod, norm_g, w_in, layer, t, dils, tail_lens, p_dtype):
    b, s, d = x.shape
    ncol = w_in.shape[2]
    n_t = s // t
    qkv_shapes = [jax.ShapeDtypeStruct((b, dil, s // dil, 3 * N_CHUNK), p_dtype) for dil in dils]
    qkv_specs = [pl.BlockSpec((None, dil, t // dil, 3 * N_CHUNK), lambda i, j: (i, 0, j, 0)) for dil in dils]
    tail_shapes, tail_specs = [], []
    for g in range(N_GROUPS):
        wlen = tail_lens[g]
        rows = min(wlen, t)
        first = n_t - wlen // rows
        tail_shapes.append(jax.ShapeDtypeStruct((b, wlen, 2 * N_CHUNK), F32))
        tail_specs.append(pl.BlockSpec((None, rows, 2 * N_CHUNK),
                                       lambda i, j, first=first: (i, jnp.maximum(j - first, 0), 0)))
    head_dim = d // N_HEADS
    return pl.pallas_call(
        functools.partial(_attproj_body, att_scale=head_dim ** -0.5),
        grid=(b, n_t),
        in_specs=[pl.BlockSpec((1, t, d), lambda i, j: (i, j, 0)),
                  _mod_spec(mod, 1, t),
                  _const_spec((1, d)),
                  _layer_spec(w_in, layer)],
        out_specs=qkv_specs + [pl.BlockSpec((None, t, N_CHUNK), lambda i, j: (i, j, 0))] + tail_specs,
        out_shape=qkv_shapes + [jax.ShapeDtypeStruct((b, s, N_CHUNK), p_dtype)] + tail_shapes,
        scratch_shapes=[pltpu.VMEM((N_CHUNK // LANES, t, LANES), F32)],
        compiler_params=_params(2),
        name="att_proj",
    )(x, mod, norm_g.reshape(1, -1), w_in)


def _slope(h):
    return 2.0 ** (-8.0 * (h + 1) / N_HEADS)


def _stat_lane(h):
    return h


def _pattn_body(q_ref, kh_ref, kc_ref, vh_ref, vc_ref, o_ref, m_ref, l_ref, k_sc, v_sc, pen_sc, *, dil, n):
    qb = q_ref.shape[0]

    @pl.when((pl.program_id(0) == 0) & (pl.program_id(1) == 0) & (pl.program_id(2) == 0))
    def _():
        ii = lax.broadcasted_iota(jnp.int32, (n, 2 * n), 0)
        jj = lax.broadcasted_iota(jnp.int32, (n, 2 * n), 1)
        diff = ii - jj + n
        dist = jnp.where((diff >= 0) & (diff <= n), (diff * dil).astype(F32), MASK_DIST)
        dist_first = jnp.where(jj >= n, dist, MASK_DIST)
        for h in range(N_HEADS):
            pen_sc[0, h] = _slope(h) * dist
            pen_sc[1, h] = _slope(h) * dist_first

    k_sc[:n, :] = kh_ref[...]
    k_sc[n:, :] = kc_ref[...]
    v_sc[:n, :] = vh_ref[...]
    v_sc[n:, :] = vc_ref[...]
    lane = lax.broadcasted_iota(jnp.int32, (n, LANES), 1)
    low = lane < (LANES // 2)
    ones = jnp.ones((2 * n, LANES), BF16)
    nt = (((1,), (1,)), ((), ()))
    for sc in range(qb // n):
        r0 = sc * n
        table = jnp.where(pl.program_id(2) > 0, 0, 1) if sc == 0 else 0
        m_all = jnp.zeros((n, LANES), F32)
        l_all = jnp.ones((n, LANES), F32)
        for p in range(N_HEADS // 2):
            cols = slice(p * LANES, (p + 1) * LANES)
            q2 = q_ref[r0:r0 + n, cols]
            k2 = k_sc[r0:r0 + 2 * n, cols]
            v2 = jnp.concatenate([v_sc[r0:r0 + 2 * n, cols], ones], axis=1)
            outs = []
            for half in range(2):
                h = 2 * p + half
                qm = jnp.where(low if half == 0 else ~low, q2, jnp.zeros_like(q2))
                s = lax.dot_general(qm, k2, nt, preferred_element_type=F32) - pen_sc[table, h]
                m = jnp.max(s, axis=-1, keepdims=True)
                e = jnp.exp(s - m).astype(BF16)
                ov = jnp.dot(e, v2, preferred_element_type=F32)
                outs.append(ov[:, :LANES])
                m_all = jnp.where(lane == h, m, m_all)
                l_all = jnp.where(lane == h, ov[:, LANES:], l_all)
            o_ref[r0:r0 + n, cols] = jnp.where(low, outs[0], outs[1]).astype(o_ref.dtype)
        m_ref[r0:r0 + n, :] = m_all
        l_ref[r0:r0 + n, :] = l_all


def _prompt_attn(qkv, g):
    b, dil, steps, _ = qkv.shape
    n = WINDOWS[g] // dil
    qb = min(steps, ATTN_QUERY_BLOCK)
    cur = lambda col: (lambda i, r, c: (i, r, c, col))
    halo = lambda col: (lambda i, r, c: (i, r, jnp.maximum(c * (qb // n) - 1, 0), col))
    cur_spec = lambda col: pl.BlockSpec((None, None, qb, N_CHUNK), cur(col))
    halo_spec = lambda col: pl.BlockSpec((None, None, n, N_CHUNK), halo(col))
    return pl.pallas_call(
        functools.partial(_pattn_body, dil=dil, n=n),
        grid=(b, dil, steps // qb),
        in_specs=[cur_spec(0), halo_spec(1), cur_spec(1), halo_spec(2), cur_spec(2)],
        out_specs=[pl.BlockSpec((None, None, qb, N_CHUNK), cur(0)),
                   pl.BlockSpec((None, None, qb, LANES), cur(0)),
                   pl.BlockSpec((None, None, qb, LANES), cur(0))],
        out_shape=[jax.ShapeDtypeStruct((b, dil, steps, N_CHUNK), BF16),
                   jax.ShapeDtypeStruct((b, dil, steps, LANES), F32),
                   jax.ShapeDtypeStruct((b, dil, steps, LANES), F32)],
        scratch_shapes=[pltpu.VMEM((qb + n, N_CHUNK), BF16), pltpu.VMEM((qb + n, N_CHUNK), BF16),
                        pltpu.VMEM((2, N_HEADS, n, 2 * n), F32)],
        compiler_params=_params(3),
        name=f"prompt_attn_g{g}",
    )(qkv, qkv, qkv, qkv, qkv)


def _sattn_body(p_ref, cache_ref, o_ref, m_ref, l_ref, m_sc, l_sc, acc_sc, *, dil, window):
    nb, t, _ = p_ref.shape
    pc = cache_ref.shape[2]
    hd = N_CHUNK // N_HEADS
    rows = N_HEADS * t
    j = pl.program_id(1)
    buf_len = pc * pl.num_programs(1)
    nt = (((1,), (1,)), ((), ()))

    own_head = (lax.broadcasted_iota(jnp.int32, (rows, N_CHUNK), 0) // t
                == lax.broadcasted_iota(jnp.int32, (rows, N_CHUNK), 1) // hd)
    ri = lax.broadcasted_iota(jnp.int32, (rows, 1), 0)
    head = ri // t
    tq = ri - head * t
    slope = jnp.exp((head + 1).astype(F32) * (-8.0 / N_HEADS * math.log(2.0)))

    def block_diag_q(bi):
        q = p_ref[bi, :, :N_CHUNK]
        qbd = jnp.where(own_head, jnp.broadcast_to(q[None], (N_HEADS, t, N_CHUNK)).reshape(rows, N_CHUNK), 0.0)
        return qbd.astype(BF16)

    qbds = [block_diag_q(bi) for bi in range(nb)]

    @pl.when(j == 0)
    def _():
        pad = jnp.zeros((LANES - t, N_CHUNK), F32)
        dist = tq - lax.broadcasted_iota(jnp.int32, (rows, LANES), 1)
        ok = (dist >= 0) & ((dist & (dil - 1)) == 0)
        pen = slope * dist.astype(F32)
        for bi in range(nb):
            kn = jnp.concatenate([p_ref[bi, :, N_CHUNK:2 * N_CHUNK], pad], axis=0).astype(BF16)
            vn = jnp.concatenate([p_ref[bi, :, 2 * N_CHUNK:], pad], axis=0).astype(BF16)
            s = lax.dot_general(qbds[bi], kn, nt, preferred_element_type=F32)
            s = jnp.where(ok, s - pen, NEG_INF)
            m = jnp.max(s, axis=-1, keepdims=True)
            e = jnp.exp(s - m)
            m_sc[bi] = m
            l_sc[bi] = jnp.sum(e, axis=-1, keepdims=True)
            acc_sc[bi] = jnp.dot(e.astype(BF16), vn, preferred_element_type=F32)

    pos = j * pc + lax.broadcasted_iota(jnp.int32, (rows, pc), 1)
    dist = buf_len + tq - pos
    ok = ((dist & (dil - 1)) == 0) & (dist <= window)
    pen = slope * dist.astype(F32)
    for bi in range(nb):
        kt = cache_ref[bi, :N_CHUNK, :].astype(BF16)
        vt = cache_ref[bi, N_CHUNK:, :].astype(BF16)
        s = jnp.dot(qbds[bi], kt, preferred_element_type=F32)
        s = jnp.where(ok, s - pen, NEG_INF)
        m_old = m_sc[bi]
        m_new = jnp.maximum(m_old, jnp.max(s, axis=-1, keepdims=True))
        alpha = jnp.exp(m_old - m_new)
        e = jnp.exp(s - m_new)
        l_sc[bi] = alpha * l_sc[bi] + jnp.sum(e, axis=-1, keepdims=True)
        acc_sc[bi] = alpha * acc_sc[bi] + lax.dot_general(e.astype(BF16), vt, nt, preferred_element_type=F32)
        m_sc[bi] = m_new

    @pl.when(j == pl.num_programs(1) - 1)
    def _():
        on_lane = (_stat_lane(lax.broadcasted_iota(jnp.int32, (rows, LANES), 0) // t)
                   == lax.broadcasted_iota(jnp.int32, (rows, LANES), 1))
        fold = lambda v: jnp.sum(v.reshape(N_HEADS, t, v.shape[-1]), axis=0)
        used = fold(on_lane.astype(F32))
        for bi in range(nb):
            o_ref[bi] = fold(jnp.where(own_head, acc_sc[bi], 0.0))
            m_ref[bi] = fold(jnp.where(on_lane, m_sc[bi], 0.0))
            l_ref[bi] = fold(jnp.where(on_lane, l_sc[bi], 0.0)) + (1.0 - used)


def _sample_attn(qkv, cache, layer, g):
    b, t, _ = qkv.shape
    natt, _, blen = cache.shape[:3]
    cache_t = jnp.transpose(cache, (0, 1, 3, 4, 5, 2)).reshape(natt, b, 2 * N_CHUNK, blen)
    pc = min(blen, SAMPLE_POS_CHUNK)
    nb = max(1, min(b, SAMPLE_CACHE_BLOCK_BYTES // (2 * N_CHUNK * pc * 4)))
    per_b = lambda i, j: (i, 0, 0)
    return pl.pallas_call(
        functools.partial(_sattn_body, dil=DILATIONS[g], window=WINDOWS[g]),
        grid=(b // nb, blen // pc),
        in_specs=[pl.BlockSpec((nb, t, 3 * N_CHUNK), per_b),
                  pl.BlockSpec((None, nb, 2 * N_CHUNK, pc), lambda i, j: (layer, i, 0, j))],
        out_specs=[pl.BlockSpec((nb, t, N_CHUNK), per_b),
                   pl.BlockSpec((nb, t, LANES), per_b),
                   pl.BlockSpec((nb, t, LANES), per_b)],
        out_shape=[jax.ShapeDtypeStruct((b, t, N_CHUNK), F32),
                   jax.ShapeDtypeStruct((b, t, LANES), F32),
                   jax.ShapeDtypeStruct((b, t, LANES), F32)],
        scratch_shapes=[pltpu.VMEM((nb, N_HEADS * t, 1), F32), pltpu.VMEM((nb, N_HEADS * t, 1), F32),
                        pltpu.VMEM((nb, N_HEADS * t, N_CHUNK), F32)],
        compiler_params=_params(2),
        name=f"sample_attn_g{g}",
    )(qkv, cache_t)


def _natural_rows(ref, stage_sc):
    dil, n, c = ref.shape
    if dil == 1:
        return ref[0].astype(F32)
    nblk = c // LANES
    for r in range(dil):
        rows = ref[r].astype(F32)
        for cb in range(nblk):
            stage_sc[cb, pl.ds(r, n, stride=dil), :] = rows[:, cb * LANES:(cb + 1) * LANES]
    return jnp.concatenate([stage_sc[cb] for cb in range(nblk)], axis=1)


def _attout_body(x_ref, mod_ref, gate_ref, o0_ref, o1_ref, o2_ref, m0_ref, m1_ref, m2_ref,
                 l0_ref, l1_ref, l2_ref, wout_ref, fg_ref, y_ref, stage_sc, *, final_norm):
    _, t, d = x_ref.shape
    hd = d // N_HEADS
    ms = [_natural_rows(r, stage_sc) for r in (m0_ref, m1_ref, m2_ref)]
    ls = [_natural_rows(r, stage_sc) for r in (l0_ref, l1_ref, l2_ref)]
    m = jnp.maximum(jnp.maximum(ms[0], ms[1]), ms[2])
    es = [jnp.exp(mg - m) for mg in ms]
    den = es[0] * ls[0] + es[1] * ls[1] + es[2] * ls[2]
    expand = (lax.broadcasted_iota(jnp.int32, (2 * LANES, d), 0) % LANES
              == _stat_lane(lax.broadcasted_iota(jnp.int32, (2 * LANES, d), 1) // hd)).astype(BF16)
    acc = jnp.zeros((t, d), F32)
    for e, o_ref in zip(es, (o0_ref, o1_ref, o2_ref)):
        w = e / den
        hi = w.astype(BF16)
        lo = (w - hi.astype(F32)).astype(BF16)
        wfull = jnp.dot(jnp.concatenate([hi, lo], axis=1), expand, preferred_element_type=F32)
        acc = acc + wfull * _natural_rows(o_ref, stage_sc)
    y = (acc * _silu(gate_ref[...].astype(F32))).astype(BF16)
    out = jnp.dot(y, wout_ref[...], preferred_element_type=F32)
    xn = x_ref[0] + mod_ref[0][:, 2 * d:] * out
    if final_norm:
        xn = xn * lax.rsqrt(jnp.mean(xn * xn, axis=-1, keepdims=True) + EPS) * fg_ref[...]
    y_ref[0] = xn


def _att_out(x, mod, gate, os_, ms, ls, w_out, layer, final_g, final_norm, t):
    b, s, d = x.shape
    blk = lambda i, j: (i, j, 0)
    res_spec = lambda v: pl.BlockSpec((None, v.shape[1], t // v.shape[1], v.shape[3]), lambda i, j: (i, 0, j, 0))
    return pl.pallas_call(
        functools.partial(_attout_body, final_norm=final_norm),
        grid=(b, s // t),
        in_specs=[pl.BlockSpec((1, t, d), blk),
                  _mod_spec(mod, 1, t),
                  pl.BlockSpec((None, t, N_CHUNK), blk)]
                 + [res_spec(v) for v in (*os_, *ms, *ls)]
                 + [_layer_spec(w_out, layer), _const_spec((1, d))],
        out_specs=pl.BlockSpec((1, t, d), blk),
        out_shape=jax.ShapeDtypeStruct((b, s, d), F32),
        scratch_shapes=[pltpu.VMEM((N_CHUNK // LANES, t, LANES), F32)],
        compiler_params=_params(2),
        name="att_out",
    )(x, mod, gate, *os_, *ms, *ls, w_out, final_g.reshape(1, -1))


def kernel(x_prompt, x_sample, state_conv, state_h, cache_kv_g0, cache_kv_g1, cache_kv_g2, c_prompt, c_sample, norm_g, ada_w, ada_b, final_g, lru_w_in, lru_conv_w, lru_conv_b, lru_wa, lru_ba, lru_wx, lru_bx, lru_lambda, lru_w_out, att_w_in, att_w_out):
    bp, sp, d = x_prompt.shape
    bs, ts, _ = x_sample.shape
    r = lru_conv_w.shape[-1]
    hd = d // N_HEADS
    caches = (cache_kv_g0, cache_kv_g1, cache_kv_g2)

    n_c = bp + bs
    c_all = jnp.concatenate([c_prompt, c_sample, jnp.zeros((-n_c % 16, d), F32)], axis=0)
    mod = _adaln_mod(c_all, ada_w, ada_b)
    mod_p = mod[:, :bp].reshape(DEPTH, bp, 1, 3 * d)
    mod_s = mod[:, bp:n_c].reshape(DEPTH, bs, 1, 3 * d)

    lru_w_in, lru_w_out, att_w_in, att_w_out = (w.astype(BF16) for w in (lru_w_in, lru_w_out, att_w_in, att_w_out))
    yp, ys = x_prompt, x_sample
    p_conv, p_h, s_conv, s_h = [], [], [], []
    p_kv = [[] for _ in range(N_GROUPS)]
    s_kv = [[] for _ in range(N_GROUPS)]
    n_keep = CONV_W - 1
    for i in range(DEPTH):
        j = i // N_MIXERS
        if i % N_MIXERS == 0:
            wg = _gate_weights(lru_wa[j], lru_wx[j])
            args = (j, norm_g[i], lru_w_in, lru_conv_w[j], lru_conv_b[j], wg, lru_ba[j],
                    lru_bx[j], lru_lambda[j], lru_w_out)
            zc = jnp.zeros((bp, SUBLANES, r), F32)
            zh = jnp.zeros((bp, 1, r), F32)
            yp, cs, ht = _lru_layer(yp, mod_p[i], *args, zc, zh, 1, PROMPT_TILE)
            p_conv.append(cs[:, SUBLANES - n_keep:])
            p_h.append(ht[:, 0])
            cs0 = jnp.pad(state_conv[j], ((0, 0), (SUBLANES - n_keep, 0), (0, 0)))
            ys, cs, ht = _lru_layer(ys, mod_s[i], *args, cs0, state_h[j][:, None, :], bs, ts)
            s_conv.append(cs[:, SUBLANES - n_keep:])
            s_h.append(ht[:, 0])
        else:
            last = i == DEPTH - 1
            p_tails = [min(w, sp) for w in WINDOWS]
            *qkvs, gate, t0, t1, t2 = _att_proj(yp, mod_p[i], norm_g[i], att_w_in, j, PROMPT_TILE, DILATIONS,
                                                p_tails, BF16)
            res = [_prompt_attn(qkvs[g], g) for g in range(N_GROUPS)]
            yp = _att_out(yp, mod_p[i], gate, *zip(*res), att_w_out, j, final_g, last, PROMPT_TILE)
            for g, tail in enumerate((t0, t1, t2)):
                p_kv[g].append(tail.reshape(bp, -1, 2, N_HEADS, hd))
            n_s = bs * ts
            flat = lambda v: v.reshape(1, n_s, v.shape[-1])
            mod_rows = flat(jnp.broadcast_to(mod_s[i], (bs, ts, 3 * d)))
            *qkvs, gate, t0, t1, t2 = _att_proj(flat(ys), mod_rows, norm_g[i], att_w_in, j, n_s,
                                                (1,) * N_GROUPS, [n_s] * N_GROUPS, F32)
            res = [_sample_attn(qkvs[g].reshape(bs, ts, -1), caches[g], j, g) for g in range(N_GROUPS)]
            nat = lambda v: v.reshape(1, 1, n_s, v.shape[-1])
            res = [[nat(v) for v in group] for group in res]
            ys = _att_out(flat(ys), mod_rows, gate, *zip(*res), att_w_out, j, final_g, last, n_s).reshape(bs, ts, d)
            for g, tail in enumerate((t0, t1, t2)):
                s_kv[g].append(tail.reshape(bs, ts, 2, N_HEADS, hd))
    return (yp, ys,
            jnp.stack(p_conv), jnp.stack(p_h),
            jnp.stack(p_kv[0]), jnp.stack(p_kv[1]), jnp.stack(p_kv[2]),
            jnp.stack(s_conv), jnp.stack(s_h),
            jnp.stack(s_kv[0]), jnp.stack(s_kv[1]), jnp.stack(s_kv[2]))
```

```python
import functools
import math

import jax
import jax.numpy as jnp
from jax import lax
from jax.experimental import pallas as pl
from jax.experimental.pallas import tpu as pltpu

F32 = jnp.float32
BF16 = jnp.bfloat16

DEPTH = 4
N_MIXERS = 2
N_BLOCKS = 16
CONV_W = 4
LRU_C = 8.0
N_HEADS = 16
WINDOWS = (128, 512, 2048)
DILATIONS = (1, 4, 16)
N_GROUPS = 3
NEG_INF = -1e30
EPS = 1e-6

LANES = 128
SUBLANES = 8
VMEM_LIMIT = 56 * 1024 * 1024
PROMPT_TILE = 256
LRU_TILE = 512
ATT_OUT_TILE = 512
GATE_GROUP = 4
N_CHUNK = 1024
SAMPLE_POS_CHUNK = 2048
SAMPLE_CACHE_BLOCK_BYTES = 8 * 1024 * 1024
ATTN_QUERY_BLOCK = 1024
MASK_DIST = 1e33


def _silu(x):
    return x * jax.nn.sigmoid(x)


def _adaln_norm(x, g, shift, scale):
    y = x * lax.rsqrt(jnp.mean(x * x, axis=-1, keepdims=True) + EPS)
    return (y * g) * (1.0 + scale) + shift


def _const_spec(shape):
    nd = len(shape)
    return pl.BlockSpec(shape, lambda *_: (0,) * nd, pipeline_mode=pl.Buffered(1))


def _layer_spec(stacked, layer):
    shape = stacked.shape[1:]
    nd = len(shape)
    return pl.BlockSpec((None,) + shape, lambda *_: (layer,) + (0,) * nd, pipeline_mode=pl.Buffered(1))


def _params(n_grid):
    return pltpu.CompilerParams(dimension_semantics=("arbitrary",) * n_grid,
                                vmem_limit_bytes=VMEM_LIMIT)


def _mod_spec(mod, nb, t):
    _, mt, d3 = mod.shape
    if mt == 1:
        return pl.BlockSpec((nb, 1, d3), lambda i, j: (i, 0, 0))
    return pl.BlockSpec((nb, t, d3), lambda i, j: (i, j, 0))


def _mod_body(c_ref, w_ref, b_ref, o_ref):
    a = _silu(c_ref[...]).astype(BF16)
    o_ref[...] = jnp.dot(a, w_ref[...].astype(BF16), preferred_element_type=F32) + b_ref[...]


def _adaln_mod(c_all, ada_w, ada_b):
    r, d = c_all.shape
    depth, _, d3 = ada_w.shape
    tn = N_CHUNK
    return pl.pallas_call(
        _mod_body,
        grid=(depth, d3 // tn),
        in_specs=[pl.BlockSpec((r, d), lambda i, n: (0, 0)),
                  pl.BlockSpec((None, d, tn), lambda i, n: (i, 0, n)),
                  pl.BlockSpec((None, 1, tn), lambda i, n: (i, 0, n))],
        out_specs=pl.BlockSpec((None, r, tn), lambda i, n: (i, 0, n)),
        out_shape=jax.ShapeDtypeStruct((depth, r, d3), F32),
        compiler_params=_params(2),
        name="adaln_mod",
    )(c_all, ada_w, ada_b.reshape(depth, 1, d3))


def _lru_body(x_ref, mod_ref, g_ref, win_ref, cw_ref, cb_ref, wg_ref, ba_ref, bx_ref, lam_ref,
              wout_ref, cs0_ref, h0_ref, y_ref, cs_ref, ht_ref, prev_sc, h_sc):
    nb, t, d = x_ref.shape
    taps, r = cw_ref.shape
    assert taps == 4, "the conv below pairs its taps two by two"
    gw = wg_ref.shape[1]

    @pl.when(pl.program_id(1) == 0)
    def _():
        prev_sc[...] = cs0_ref[...]
        h_sc[...] = h0_ref[...]

    x = x_ref[...]
    mod = mod_ref[...]
    shift, scale, gate = mod[..., :d], mod[..., d:2 * d], mod[..., 2 * d:]
    hn = _adaln_norm(x, g_ref[...], shift, scale).reshape(nb * t, d).astype(BF16)
    proj = jnp.dot(hn, win_ref[...], preferred_element_type=F32)
    xb = proj[:, :r].reshape(nb, t, r)
    gb = proj[:, r:]

    cat = jnp.concatenate([prev_sc[...], xb], axis=1)
    cw = cw_ref[...]
    back1 = pltpu.roll(cat, 1, 1)
    older = pltpu.roll(cat * cw[1:2] + back1 * cw[0:1], 2, 1)
    xc = cb_ref[...] + xb * cw[3:4] + back1[:, SUBLANES:, :] * cw[2:3] + older[:, SUBLANES:, :]
    new_prev = cat[:, t:, :]
    prev_sc[...] = new_prev
    cs_ref[...] = new_prev

    xc2 = xc.reshape(nb * t, r)
    xcb = xc2.astype(BF16)
    grs, gis = [], []
    for q in range(r // gw):
        p = jnp.dot(xcb[:, q * gw:(q + 1) * gw], wg_ref[q], preferred_element_type=F32)
        grs.append(p[:, :gw])
        gis.append(p[:, gw:])
    rg = jax.nn.sigmoid(jnp.concatenate(grs, axis=1) + ba_ref[...])
    ig = jax.nn.sigmoid(jnp.concatenate(gis, axis=1) + bx_ref[...])
    nlam = -lam_ref[...]
    softplus = jnp.maximum(nlam, 0.0) + jnp.log1p(jnp.exp(-jnp.abs(nlam)))
    z = LRU_C * rg * softplus
    a = jnp.exp(-z)
    u = jnp.sqrt(jnp.tanh(z) * (1.0 + a * a)) * ig * xc2

    groups = t // SUBLANES
    a3 = a.reshape(nb * groups, SUBLANES, r)
    u3 = u.reshape(nb * groups, SUBLANES, r)
    row = lax.broadcasted_iota(jnp.int32, (1, SUBLANES, r), 1)
    dist = 1
    while dist < SUBLANES:
        keep = row >= dist
        u3 = jnp.where(keep, a3 * pltpu.roll(u3, dist, 1) + u3, u3)
        a3 = jnp.where(keep, a3 * pltpu.roll(a3, dist, 1), a3)
        dist *= 2
    a4 = a3.reshape(nb, groups, SUBLANES, r)
    u4 = u3.reshape(nb, groups, SUBLANES, r)
    h_prev = h_sc[...]
    hs = []
    for gi in range(groups):
        hs.append(a4[:, gi] * h_prev + u4[:, gi])
        h_prev = hs[-1][:, SUBLANES - 1:, :]
    h = hs[0] if groups == 1 else jnp.concatenate(hs, axis=1)
    h_last = h_prev
    h_sc[...] = h_last
    ht_ref[...] = h_last

    y = (h.reshape(nb * t, r) * _silu(gb)).astype(BF16)
    out = jnp.dot(y, wout_ref[...], preferred_element_type=F32).reshape(nb, t, d)
    y_ref[...] = x + gate * out


def _gate_weights(wa, wx):
    nblk, bw, _ = wa.shape
    ngrp = nblk // GATE_GROUP
    eye = jnp.eye(GATE_GROUP, dtype=wa.dtype)[None, :, None, :, None]

    def block_diag(w):
        w = w.reshape(ngrp, GATE_GROUP, bw, 1, bw) * eye
        return w.reshape(ngrp, GATE_GROUP * bw, GATE_GROUP * bw)

    return jnp.concatenate([block_diag(wa), block_diag(wx)], axis=-1).astype(BF16)


def _lru_layer(x, mod, layer, norm_g, w_in, conv_w, conv_b, wg, ba, bx, lam, w_out, cs0, h0, nb, t):
    b, s, d = x.shape
    r = conv_w.shape[1]
    gw = wg.shape[1]
    row = lambda v: v.reshape(1, -1)
    blk = lambda i, j: (i, j, 0)
    per_b = lambda i, j: (i, 0, 0)
    return pl.pallas_call(
        _lru_body,
        grid=(b // nb, s // t),
        in_specs=[pl.BlockSpec((nb, t, d), blk),
                  pl.BlockSpec((nb, 1, 3 * d), per_b),
                  _const_spec((1, d)),
                  _layer_spec(w_in, layer),
                  _const_spec((CONV_W, r)),
                  _const_spec((1, r)),
                  _const_spec((r // gw, gw, 2 * gw)),
                  _const_spec((1, r)),
                  _const_spec((1, r)),
                  _const_spec((1, r)),
                  _layer_spec(w_out, layer),
                  pl.BlockSpec((nb, SUBLANES, r), per_b),
                  pl.BlockSpec((nb, 1, r), per_b)],
        out_specs=[pl.BlockSpec((nb, t, d), blk),
                   pl.BlockSpec((nb, SUBLANES, r), per_b),
                   pl.BlockSpec((nb, 1, r), per_b)],
        out_shape=[jax.ShapeDtypeStruct((b, s, d), F32),
                   jax.ShapeDtypeStruct((b, SUBLANES, r), F32),
                   jax.ShapeDtypeStruct((b, 1, r), F32)],
        scratch_shapes=[pltpu.VMEM((nb, SUBLANES, r), F32), pltpu.VMEM((nb, 1, r), F32)],
        compiler_params=_params(2),
        name="lru_layer",
    )(x, mod, row(norm_g), w_in, conv_w, row(conv_b), wg, row(ba), row(bx), row(lam), w_out, cs0, h0)


def _attproj_body(x_ref, mod_ref, g_ref, w_ref, p0_ref, p1_ref, p2_ref, gate_ref, t0_ref, t1_ref, t2_ref,
                  stage_sc, *, att_scale):
    _, t, d = x_ref.shape
    mod = mod_ref[...]
    hn = _adaln_norm(x_ref[...], g_ref[...], mod[..., :d], mod[..., d:2 * d])
    hn = hn.reshape(t, d).astype(BF16)
    tails = (t0_ref, t1_ref, t2_ref)
    qkvs = (p0_ref, p1_ref, p2_ref)
    for c in range(3 * N_GROUPS + 1):
        res = jnp.dot(hn, w_ref[:, c * N_CHUNK:(c + 1) * N_CHUNK], preferred_element_type=F32)
        g, j = divmod(c, 3)
        if g == N_GROUPS:
            gate_ref[...] = res.astype(gate_ref.dtype)
            continue
        if j == 0:
            res = res * att_scale
        else:
            rows = tails[g].shape[0]
            tails[g][:, (j - 1) * N_CHUNK:j * N_CHUNK] = res[t - rows:, :]
        cols = slice(j * N_CHUNK, (j + 1) * N_CHUNK)
        dil = qkvs[g].shape[0]
        if dil == 1:
            qkvs[g][0, :, cols] = res.astype(qkvs[g].dtype)
        else:
            nblk = N_CHUNK // LANES
            for cb in range(nblk):
                stage_sc[cb] = res[:, cb * LANES:(cb + 1) * LANES]
            for r in range(dil):
                picked = [stage_sc[cb, pl.ds(r, t // dil, stride=dil), :] for cb in range(nblk)]
                qkvs[g][r, :, cols] = jnp.concatenate(picked, axis=1).astype(qkvs[g].dtype)


def _att_proj(x, mod, norm_g, w_in, layer, t, dils, tail_lens, p_dtype):
    b, s, d = x.shape
    ncol = w_in.shape[2]
    n_t = s // t
    qkv_shapes = [jax.ShapeDtypeStruct((b, dil, s // dil, 3 * N_CHUNK), p_dtype) for dil in dils]
    qkv_specs = [pl.BlockSpec((None, dil, t // dil, 3 * N_CHUNK), lambda i, j: (i, 0, j, 0)) for dil in dils]
    tail_shapes, tail_specs = [], []
    for g in range(N_GROUPS):
        wlen = tail_lens[g]
        rows = min(wlen, t)
        first = n_t - wlen // rows
        tail_shapes.append(jax.ShapeDtypeStruct((b, wlen, 2 * N_CHUNK), F32))
        tail_specs.append(pl.BlockSpec((None, rows, 2 * N_CHUNK),
                                       lambda i, j, first=first: (i, jnp.maximum(j - first, 0), 0)))
    head_dim = d // N_HEADS
    return pl.pallas_call(
        functools.partial(_attproj_body, att_scale=head_dim ** -0.5),
        grid=(b, n_t),
        in_specs=[pl.BlockSpec((1, t, d), lambda i, j: (i, j, 0)),
                  _mod_spec(mod, 1, t),
                  _const_spec((1, d)),
                  _layer_spec(w_in, layer)],
        out_specs=qkv_specs + [pl.BlockSpec((None, t, N_CHUNK), lambda i, j: (i, j, 0))] + tail_specs,
        out_shape=qkv_shapes + [jax.ShapeDtypeStruct((b, s, N_CHUNK), p_dtype)] + tail_shapes,
        scratch_shapes=[pltpu.VMEM((N_CHUNK // LANES, t, LANES), F32)],
        compiler_params=_params(2),
        name="att_proj",
    )(x, mod, norm_g.reshape(1, -1), w_in)


def _slope(h):
    return 2.0 ** (-8.0 * (h + 1) / N_HEADS)


def _stat_lane(h):
    return h


def _pattn_body(q_ref, kh_ref, kc_ref, vh_ref, vc_ref, o_ref, m_ref, l_ref, k_sc, v_sc, pen_sc, *, dil, n):
    rb, qb, _ = q_ref.shape

    @pl.when((pl.program_id(0) == 0) & (pl.program_id(1) == 0) & (pl.program_id(2) == 0))
    def _():
        ii = lax.broadcasted_iota(jnp.int32, (n, 2 * n), 0)
        jj = lax.broadcasted_iota(jnp.int32, (n, 2 * n), 1)
        diff = ii - jj + n
        dist = jnp.where((diff >= 0) & (diff <= n), (diff * dil).astype(F32), MASK_DIST)
        dist_first = jnp.where(jj >= n, dist, MASK_DIST)
        for h in range(N_HEADS):
            pen_sc[0, h] = _slope(h) * dist
            pen_sc[1, h] = _slope(h) * dist_first

    k_sc[:, :n, :] = kh_ref[...]
    k_sc[:, n:, :] = kc_ref[...]
    v_sc[:, :n, :] = vh_ref[...]
    v_sc[:, n:, :] = vc_ref[...]
    lane = lax.broadcasted_iota(jnp.int32, (n, LANES), 1)
    low = lane < (LANES // 2)
    ones = jnp.ones((2 * n, LANES), BF16)
    nt = (((1,), (1,)), ((), ()))
    for ri, sc in [(ri, sc) for ri in range(rb) for sc in range(qb // n)]:
        r0 = sc * n
        table = jnp.where(pl.program_id(2) > 0, 0, 1) if sc == 0 else 0
        m_all = jnp.zeros((n, LANES), F32)
        l_all = jnp.ones((n, LANES), F32)
        for p in range(N_HEADS // 2):
            cols = slice(p * LANES, (p + 1) * LANES)
            q2 = q_ref[ri, r0:r0 + n, cols]
            k2 = k_sc[ri, r0:r0 + 2 * n, cols]
            v2 = jnp.concatenate([v_sc[ri, r0:r0 + 2 * n, cols], ones], axis=1)
            outs = []
            for half in range(2):
                h = 2 * p + half
                qm = jnp.where(low if half == 0 else ~low, q2, jnp.zeros_like(q2))
                s = lax.dot_general(qm, k2, nt, preferred_element_type=F32) - pen_sc[table, h]
                m = jnp.max(s, axis=-1, keepdims=True)
                e = jnp.exp(s - m).astype(BF16)
                ov = jnp.dot(e, v2, preferred_element_type=F32)
                outs.append(ov[:, :LANES])
                m_all = jnp.where(lane == h, m, m_all)
                l_all = jnp.where(lane == h, ov[:, LANES:], l_all)
            o_ref[ri, r0:r0 + n, cols] = jnp.where(low, outs[0], outs[1]).astype(o_ref.dtype)
        m_ref[ri, r0:r0 + n, :] = m_all
        l_ref[ri, r0:r0 + n, :] = l_all


def _prompt_attn(qkv, g):
    b, dil, steps, _ = qkv.shape
    n = WINDOWS[g] // dil
    qb = min(steps, ATTN_QUERY_BLOCK)
    rb = min(dil, ATTN_QUERY_BLOCK // qb)
    cur = lambda col: (lambda i, r, c: (i, r, c, col))
    halo = lambda col: (lambda i, r, c: (i, r, jnp.maximum(c * (qb // n) - 1, 0), col))
    cur_spec = lambda col: pl.BlockSpec((None, rb, qb, N_CHUNK), cur(col))
    halo_spec = lambda col: pl.BlockSpec((None, rb, n, N_CHUNK), halo(col))
    return pl.pallas_call(
        functools.partial(_pattn_body, dil=dil, n=n),
        grid=(b, dil // rb, steps // qb),
        in_specs=[cur_spec(0), halo_spec(1), cur_spec(1), halo_spec(2), cur_spec(2)],
        out_specs=[pl.BlockSpec((None, rb, qb, N_CHUNK), cur(0)),
                   pl.BlockSpec((None, rb, qb, LANES), cur(0)),
                   pl.BlockSpec((None, rb, qb, LANES), cur(0))],
        out_shape=[jax.ShapeDtypeStruct((b, dil, steps, N_CHUNK), BF16),
                   jax.ShapeDtypeStruct((b, dil, steps, LANES), F32),
                   jax.ShapeDtypeStruct((b, dil, steps, LANES), F32)],
        scratch_shapes=[pltpu.VMEM((rb, qb + n, N_CHUNK), BF16), pltpu.VMEM((rb, qb + n, N_CHUNK), BF16),
                        pltpu.VMEM((2, N_HEADS, n, 2 * n), F32)],
        compiler_params=_params(3),
        name=f"prompt_attn_g{g}",
    )(qkv, qkv, qkv, qkv, qkv)


def _sattn_body(p_ref, cache_ref, o_ref, m_ref, l_ref, m_sc, l_sc, acc_sc, *, dil, window):
    nb, t, _ = p_ref.shape
    pc = cache_ref.shape[2]
    hd = N_CHUNK // N_HEADS
    rows = N_HEADS * t
    j = pl.program_id(1)
    buf_len = pc * pl.num_programs(1)
    nt = (((1,), (1,)), ((), ()))

    own_head = (lax.broadcasted_iota(jnp.int32, (rows, N_CHUNK), 0) // t
                == lax.broadcasted_iota(jnp.int32, (rows, N_CHUNK), 1) // hd)
    ri = lax.broadcasted_iota(jnp.int32, (rows, 1), 0)
    head = ri // t
    tq = ri - head * t
    slope = jnp.exp((head + 1).astype(F32) * (-8.0 / N_HEADS * math.log(2.0)))

    def block_diag_q(bi):
        q = p_ref[bi, :, :N_CHUNK]
        qbd = jnp.where(own_head, jnp.broadcast_to(q[None], (N_HEADS, t, N_CHUNK)).reshape(rows, N_CHUNK), 0.0)
        return qbd.astype(BF16)

    qbds = [block_diag_q(bi) for bi in range(nb)]

    @pl.when(j == 0)
    def _():
        pad = jnp.zeros((LANES - t, N_CHUNK), F32)
        dist = tq - lax.broadcasted_iota(jnp.int32, (rows, LANES), 1)
        ok = (dist >= 0) & ((dist & (dil - 1)) == 0)
        pen = slope * dist.astype(F32)
        for bi in range(nb):
            kn = jnp.concatenate([p_ref[bi, :, N_CHUNK:2 * N_CHUNK], pad], axis=0).astype(BF16)
            vn = jnp.concatenate([p_ref[bi, :, 2 * N_CHUNK:], pad], axis=0).astype(BF16)
            s = lax.dot_general(qbds[bi], kn, nt, preferred_element_type=F32)
            s = jnp.where(ok, s - pen, NEG_INF)
            m = jnp.max(s, axis=-1, keepdims=True)
            e = jnp.exp(s - m)
            m_sc[bi] = m
            l_sc[bi] = jnp.sum(e, axis=-1, keepdims=True)
            acc_sc[bi] = jnp.dot(e.astype(BF16), vn, preferred_element_type=F32)

    pos = j * pc + lax.broadcasted_iota(jnp.int32, (rows, pc), 1)
    dist = buf_len + tq - pos
    ok = ((dist & (dil - 1)) == 0) & (dist <= window)
    pen = slope * dist.astype(F32)
    for bi in range(nb):
        kt = cache_ref[bi, :N_CHUNK, :].astype(BF16)
        vt = cache_ref[bi, N_CHUNK:, :].astype(BF16)
        s = jnp.dot(qbds[bi], kt, preferred_element_type=F32)
        s = jnp.where(ok, s - pen, NEG_INF)
        m_old = m_sc[bi]
        m_new = jnp.maximum(m_old, jnp.max(s, axis=-1, keepdims=True))
        alpha = jnp.exp(m_old - m_new)
        e = jnp.exp(s - m_new)
        l_sc[bi] = alpha * l_sc[bi] + jnp.sum(e, axis=-1, keepdims=True)
        acc_sc[bi] = alpha * acc_sc[bi] + lax.dot_general(e.astype(BF16), vt, nt, preferred_element_type=F32)
        m_sc[bi] = m_new

    @pl.when(j == pl.num_programs(1) - 1)
    def _():
        on_lane = (_stat_lane(lax.broadcasted_iota(jnp.int32, (rows, LANES), 0) // t)
                   == lax.broadcasted_iota(jnp.int32, (rows, LANES), 1))
        fold = lambda v: jnp.sum(v.reshape(N_HEADS, t, v.shape[-1]), axis=0)
        used = fold(on_lane.astype(F32))
        for bi in range(nb):
            o_ref[bi] = fold(jnp.where(own_head, acc_sc[bi], 0.0))
            m_ref[bi] = fold(jnp.where(on_lane, m_sc[bi], 0.0))
            l_ref[bi] = fold(jnp.where(on_lane, l_sc[bi], 0.0)) + (1.0 - used)


def _sample_attn(qkv, cache, layer, g):
    b, t, _ = qkv.shape
    natt, _, blen = cache.shape[:3]
    cache_t = jnp.transpose(cache, (0, 1, 3, 4, 5, 2)).reshape(natt, b, 2 * N_CHUNK, blen)
    pc = min(blen, SAMPLE_POS_CHUNK)
    nb = max(1, min(b, SAMPLE_CACHE_BLOCK_BYTES // (2 * N_CHUNK * pc * 4)))
    per_b = lambda i, j: (i, 0, 0)
    return pl.pallas_call(
        functools.partial(_sattn_body, dil=DILATIONS[g], window=WINDOWS[g]),
        grid=(b // nb, blen // pc),
        in_specs=[pl.BlockSpec((nb, t, 3 * N_CHUNK), per_b),
                  pl.BlockSpec((None, nb, 2 * N_CHUNK, pc), lambda i, j: (layer, i, 0, j))],
        out_specs=[pl.BlockSpec((nb, t, N_CHUNK), per_b),
                   pl.BlockSpec((nb, t, LANES), per_b),
                   pl.BlockSpec((nb, t, LANES), per_b)],
        out_shape=[jax.ShapeDtypeStruct((b, t, N_CHUNK), F32),
                   jax.ShapeDtypeStruct((b, t, LANES), F32),
                   jax.ShapeDtypeStruct((b, t, LANES), F32)],
        scratch_shapes=[pltpu.VMEM((nb, N_HEADS * t, 1), F32), pltpu.VMEM((nb, N_HEADS * t, 1), F32),
                        pltpu.VMEM((nb, N_HEADS * t, N_CHUNK), F32)],
        compiler_params=_params(2),
        name=f"sample_attn_g{g}",
    )(qkv, cache_t)


def _natural_rows(ref, stage_sc):
    dil, n, c = ref.shape
    if dil == 1:
        return ref[0].astype(F32)
    nblk = c // LANES
    for r in range(dil):
        rows = ref[r].astype(F32)
        for cb in range(nblk):
            stage_sc[cb, pl.ds(r, n, stride=dil), :] = rows[:, cb * LANES:(cb + 1) * LANES]
    return jnp.concatenate([stage_sc[cb] for cb in range(nblk)], axis=1)


def _attout_body(x_ref, mod_ref, gate_ref, o0_ref, o1_ref, o2_ref, m0_ref, m1_ref, m2_ref,
                 l0_ref, l1_ref, l2_ref, wout_ref, fg_ref, y_ref, stage_sc, *, final_norm):
    _, t, d = x_ref.shape
    hd = d // N_HEADS
    ms = [_natural_rows(r, stage_sc) for r in (m0_ref, m1_ref, m2_ref)]
    ls = [_natural_rows(r, stage_sc) for r in (l0_ref, l1_ref, l2_ref)]
    m = jnp.maximum(jnp.maximum(ms[0], ms[1]), ms[2])
    es = [jnp.exp(mg - m) for mg in ms]
    den = es[0] * ls[0] + es[1] * ls[1] + es[2] * ls[2]
    expand = (lax.broadcasted_iota(jnp.int32, (2 * LANES, d), 0) % LANES
              == _stat_lane(lax.broadcasted_iota(jnp.int32, (2 * LANES, d), 1) // hd)).astype(BF16)
    acc = jnp.zeros((t, d), F32)
    for e, o_ref in zip(es, (o0_ref, o1_ref, o2_ref)):
        w = e / den
        hi = w.astype(BF16)
        lo = (w - hi.astype(F32)).astype(BF16)
        wfull = jnp.dot(jnp.concatenate([hi, lo], axis=1), expand, preferred_element_type=F32)
        acc = acc + wfull * _natural_rows(o_ref, stage_sc)
    y = (acc * _silu(gate_ref[...].astype(F32))).astype(BF16)
    out = jnp.dot(y, wout_ref[...], preferred_element_type=F32)
    xn = x_ref[0] + mod_ref[0][:, 2 * d:] * out
    if final_norm:
        xn = xn * lax.rsqrt(jnp.mean(xn * xn, axis=-1, keepdims=True) + EPS) * fg_ref[...]
    y_ref[0] = xn


def _att_out(x, mod, gate, os_, ms, ls, w_out, layer, final_g, final_norm, t):
    b, s, d = x.shape
    blk = lambda i, j: (i, j, 0)
    res_spec = lambda v: pl.BlockSpec((None, v.shape[1], t // v.shape[1], v.shape[3]), lambda i, j: (i, 0, j, 0))
    return pl.pallas_call(
        functools.partial(_attout_body, final_norm=final_norm),
        grid=(b, s // t),
        in_specs=[pl.BlockSpec((1, t, d), blk),
                  _mod_spec(mod, 1, t),
                  pl.BlockSpec((None, t, N_CHUNK), blk)]
                 + [res_spec(v) for v in (*os_, *ms, *ls)]
                 + [_layer_spec(w_out, layer), _const_spec((1, d))],
        out_specs=pl.BlockSpec((1, t, d), blk),
        out_shape=jax.ShapeDtypeStruct((b, s, d), F32),
        scratch_shapes=[pltpu.VMEM((N_CHUNK // LANES, t, LANES), F32)],
        compiler_params=_params(2),
        name="att_out",
    )(x, mod, gate, *os_, *ms, *ls, w_out, final_g.reshape(1, -1))


def kernel(x_prompt, x_sample, state_conv, state_h, cache_kv_g0, cache_kv_g1, cache_kv_g2, c_prompt, c_sample, norm_g, ada_w, ada_b, final_g, lru_w_in, lru_conv_w, lru_conv_b, lru_wa, lru_ba, lru_wx, lru_bx, lru_lambda, lru_w_out, att_w_in, att_w_out):
    bp, sp, d = x_prompt.shape
    bs, ts, _ = x_sample.shape
    r = lru_conv_w.shape[-1]
    hd = d // N_HEADS
    caches = (cache_kv_g0, cache_kv_g1, cache_kv_g2)

    n_c = bp + bs
    c_all = jnp.concatenate([c_prompt, c_sample, jnp.zeros((-n_c % 16, d), F32)], axis=0)
    mod = _adaln_mod(c_all, ada_w, ada_b)
    mod_p = mod[:, :bp].reshape(DEPTH, bp, 1, 3 * d)
    mod_s = mod[:, bp:n_c].reshape(DEPTH, bs, 1, 3 * d)

    lru_w_in, lru_w_out, att_w_in, att_w_out = (w.astype(BF16) for w in (lru_w_in, lru_w_out, att_w_in, att_w_out))
    yp, ys = x_prompt, x_sample
    p_conv, p_h, s_conv, s_h = [], [], [], []
    p_kv = [[] for _ in range(N_GROUPS)]
    s_kv = [[] for _ in range(N_GROUPS)]
    n_keep = CONV_W - 1
    for i in range(DEPTH):
        j = i // N_MIXERS
        if i % N_MIXERS == 0:
            wg = _gate_weights(lru_wa[j], lru_wx[j])
            args = (j, norm_g[i], lru_w_in, lru_conv_w[j], lru_conv_b[j], wg, lru_ba[j],
                    lru_bx[j], lru_lambda[j], lru_w_out)
            zc = jnp.zeros((bp, SUBLANES, r), F32)
            zh = jnp.zeros((bp, 1, r), F32)
            yp, cs, ht = _lru_layer(yp, mod_p[i], *args, zc, zh, 1, LRU_TILE)
            p_conv.append(cs[:, SUBLANES - n_keep:])
            p_h.append(ht[:, 0])
            cs0 = jnp.pad(state_conv[j], ((0, 0), (SUBLANES - n_keep, 0), (0, 0)))
            ys, cs, ht = _lru_layer(ys, mod_s[i], *args, cs0, state_h[j][:, None, :], bs, ts)
            s_conv.append(cs[:, SUBLANES - n_keep:])
            s_h.append(ht[:, 0])
        else:
            last = i == DEPTH - 1
            p_tails = [min(w, sp) for w in WINDOWS]
            *qkvs, gate, t0, t1, t2 = _att_proj(yp, mod_p[i], norm_g[i], att_w_in, j, PROMPT_TILE, DILATIONS,
                                                p_tails, BF16)
            res = [_prompt_attn(qkvs[g], g) for g in range(N_GROUPS)]
            yp = _att_out(yp, mod_p[i], gate, *zip(*res), att_w_out, j, final_g, last, ATT_OUT_TILE)
            for g, tail in enumerate((t0, t1, t2)):
                p_kv[g].append(tail.reshape(bp, -1, 2, N_HEADS, hd))
            n_s = bs * ts
            flat = lambda v: v.reshape(1, n_s, v.shape[-1])
            mod_rows = flat(jnp.broadcast_to(mod_s[i], (bs, ts, 3 * d)))
            *qkvs, gate, t0, t1, t2 = _att_proj(flat(ys), mod_rows, norm_g[i], att_w_in, j, n_s,
                                                (1,) * N_GROUPS, [n_s] * N_GROUPS, F32)
            res = [_sample_attn(qkvs[g].reshape(bs, ts, -1), caches[g], j, g) for g in range(N_GROUPS)]
            nat = lambda v: v.reshape(1, 1, n_s, v.shape[-1])
            res = [[nat(v) for v in group] for group in res]
            ys = _att_out(flat(ys), mod_rows, gate, *zip(*res), att_w_out, j, final_g, last, n_s).reshape(bs, ts, d)
            for g, tail in enumerate((t0, t1, t2)):
                s_kv[g].append(tail.reshape(bs, ts, 2, N_HEADS, hd))
    return (yp, ys,
            jnp.stack(p_conv), jnp.stack(p_h),
            jnp.stack(p_kv[0]), jnp.stack(p_kv[1]), jnp.stack(p_kv[2]),
            jnp.stack(s_conv), jnp.stack(s_h),
            jnp.stack(s_kv[0]), jnp.stack(s_kv[1]), jnp.stack(s_kv[2]))
```

```python
import functools
import math

import jax
import jax.numpy as jnp
from jax import lax
from jax.experimental import pallas as pl
from jax.experimental.pallas import tpu as pltpu

F32 = jnp.float32
BF16 = jnp.bfloat16

DEPTH = 4
N_MIXERS = 2
N_BLOCKS = 16
CONV_W = 4
LRU_C = 8.0
N_HEADS = 16
WINDOWS = (128, 512, 2048)
DILATIONS = (1, 4, 16)
N_GROUPS = 3
NEG_INF = -1e30
EPS = 1e-6

LANES = 128
SUBLANES = 8
VMEM_LIMIT = 56 * 1024 * 1024
PROMPT_TILE = 256
LRU_TILE = 512
ATT_OUT_TILE = 512
GATE_GROUP = 4
N_CHUNK = 1024
SAMPLE_POS_CHUNK = 2048
SAMPLE_CACHE_BLOCK_BYTES = 8 * 1024 * 1024
ATTN_QUERY_BLOCK = 1024
MASK_DIST = 1e33


def _silu(x):
    return x * jax.nn.sigmoid(x)


def _adaln_norm(x, g, shift, scale):
    y = x * lax.rsqrt(jnp.mean(x * x, axis=-1, keepdims=True) + EPS)
    return (y * g) * (1.0 + scale) + shift


def _const_spec(shape):
    nd = len(shape)
    return pl.BlockSpec(shape, lambda *_: (0,) * nd, pipeline_mode=pl.Buffered(1))


def _layer_spec(stacked, layer):
    shape = stacked.shape[1:]
    nd = len(shape)
    return pl.BlockSpec((None,) + shape, lambda *_: (layer,) + (0,) * nd, pipeline_mode=pl.Buffered(1))


def _params(n_grid):
    return pltpu.CompilerParams(dimension_semantics=("arbitrary",) * n_grid,
                                vmem_limit_bytes=VMEM_LIMIT)


def _mod_spec(mod, nb, t):
    _, mt, d3 = mod.shape
    if mt == 1:
        return pl.BlockSpec((nb, 1, d3), lambda i, j: (i, 0, 0))
    return pl.BlockSpec((nb, t, d3), lambda i, j: (i, j, 0))


def _mod_body(c_ref, w_ref, b_ref, o_ref):
    a = _silu(c_ref[...]).astype(BF16)
    o_ref[...] = jnp.dot(a, w_ref[...].astype(BF16), preferred_element_type=F32) + b_ref[...]


def _adaln_mod(c_all, ada_w, ada_b):
    r, d = c_all.shape
    depth, _, d3 = ada_w.shape
    tn = N_CHUNK
    return pl.pallas_call(
        _mod_body,
        grid=(depth, d3 // tn),
        in_specs=[pl.BlockSpec((r, d), lambda i, n: (0, 0)),
                  pl.BlockSpec((None, d, tn), lambda i, n: (i, 0, n)),
                  pl.BlockSpec((None, 1, tn), lambda i, n: (i, 0, n))],
        out_specs=pl.BlockSpec((None, r, tn), lambda i, n: (i, 0, n)),
        out_shape=jax.ShapeDtypeStruct((depth, r, d3), F32),
        compiler_params=_params(2),
        name="adaln_mod",
    )(c_all, ada_w, ada_b.reshape(depth, 1, d3))


def _lru_body(x_ref, mod_ref, g_ref, win_ref, cw_ref, cb_ref, wg_ref, ba_ref, bx_ref, lam_ref,
              wout_ref, cs0_ref, h0_ref, y_ref, cs_ref, ht_ref, prev_sc, h_sc):
    nb, t, d = x_ref.shape
    taps, r = cw_ref.shape
    assert taps == 4, "the conv below pairs its taps two by two"
    gw = wg_ref.shape[1]

    @pl.when(pl.program_id(1) == 0)
    def _():
        prev_sc[...] = cs0_ref[...]
        h_sc[...] = h0_ref[...]

    x = x_ref[...]
    mod = mod_ref[...]
    shift, scale, gate = mod[..., :d], mod[..., d:2 * d], mod[..., 2 * d:]
    hn = _adaln_norm(x, g_ref[...], shift, scale).reshape(nb * t, d).astype(BF16)
    proj = jnp.dot(hn, win_ref[...], preferred_element_type=F32)
    xb = proj[:, :r].reshape(nb, t, r)
    gb = proj[:, r:]

    cat = jnp.concatenate([prev_sc[...], xb], axis=1)
    cw = cw_ref[...]
    back1 = pltpu.roll(cat, 1, 1)
    older = pltpu.roll(cat * cw[1:2] + back1 * cw[0:1], 2, 1)
    xc = cb_ref[...] + xb * cw[3:4] + back1[:, SUBLANES:, :] * cw[2:3] + older[:, SUBLANES:, :]
    new_prev = cat[:, t:, :]
    prev_sc[...] = new_prev
    cs_ref[...] = new_prev

    xc2 = xc.reshape(nb * t, r)
    xcb = xc2.astype(BF16)
    grs, gis = [], []
    for q in range(r // gw):
        p = jnp.dot(xcb[:, q * gw:(q + 1) * gw], wg_ref[q], preferred_element_type=F32)
        grs.append(p[:, :gw])
        gis.append(p[:, gw:])
    rg = jax.nn.sigmoid(jnp.concatenate(grs, axis=1) + ba_ref[...])
    ig = jax.nn.sigmoid(jnp.concatenate(gis, axis=1) + bx_ref[...])
    nlam = -lam_ref[...]
    softplus = jnp.maximum(nlam, 0.0) + jnp.log1p(jnp.exp(-jnp.abs(nlam)))
    z = LRU_C * rg * softplus
    a = jnp.exp(-z)
    u = jnp.sqrt(jnp.tanh(z) * (1.0 + a * a)) * ig * xc2

    groups = t // SUBLANES
    a3 = a.reshape(nb * groups, SUBLANES, r)
    u3 = u.reshape(nb * groups, SUBLANES, r)
    row = lax.broadcasted_iota(jnp.int32, (1, SUBLANES, r), 1)
    dist = 1
    while dist < SUBLANES:
        keep = row >= dist
        u3 = jnp.where(keep, a3 * pltpu.roll(u3, dist, 1) + u3, u3)
        a3 = jnp.where(keep, a3 * pltpu.roll(a3, dist, 1), a3)
        dist *= 2
    a4 = a3.reshape(nb, groups, SUBLANES, r)
    u4 = u3.reshape(nb, groups, SUBLANES, r)
    h_prev = h_sc[...]
    hs = []
    for gi in range(groups):
        hs.append(a4[:, gi] * h_prev + u4[:, gi])
        h_prev = hs[-1][:, SUBLANES - 1:, :]
    h = hs[0] if groups == 1 else jnp.concatenate(hs, axis=1)
    h_last = h_prev
    h_sc[...] = h_last
    ht_ref[...] = h_last

    y = (h.reshape(nb * t, r) * _silu(gb)).astype(BF16)
    out = jnp.dot(y, wout_ref[...], preferred_element_type=F32).reshape(nb, t, d)
    y_ref[...] = x + gate * out


def _gate_weights(wa, wx):
    nblk, bw, _ = wa.shape
    ngrp = nblk // GATE_GROUP
    eye = jnp.eye(GATE_GROUP, dtype=wa.dtype)[None, :, None, :, None]

    def block_diag(w):
        w = w.reshape(ngrp, GATE_GROUP, bw, 1, bw) * eye
        return w.reshape(ngrp, GATE_GROUP * bw, GATE_GROUP * bw)

    return jnp.concatenate([block_diag(wa), block_diag(wx)], axis=-1).astype(BF16)


def _lru_layer(x, mod, layer, norm_g, w_in, conv_w, conv_b, wg, ba, bx, lam, w_out, cs0, h0, nb, t):
    b, s, d = x.shape
    r = conv_w.shape[1]
    gw = wg.shape[1]
    row = lambda v: v.reshape(1, -1)
    blk = lambda i, j: (i, j, 0)
    per_b = lambda i, j: (i, 0, 0)
    return pl.pallas_call(
        _lru_body,
        grid=(b // nb, s // t),
        in_specs=[pl.BlockSpec((nb, t, d), blk),
                  pl.BlockSpec((nb, 1, 3 * d), per_b),
                  _const_spec((1, d)),
                  _layer_spec(w_in, layer),
                  _const_spec((CONV_W, r)),
                  _const_spec((1, r)),
                  _const_spec((r // gw, gw, 2 * gw)),
                  _const_spec((1, r)),
                  _const_spec((1, r)),
                  _const_spec((1, r)),
                  _layer_spec(w_out, layer),
                  pl.BlockSpec((nb, SUBLANES, r), per_b),
                  pl.BlockSpec((nb, 1, r), per_b)],
        out_specs=[pl.BlockSpec((nb, t, d), blk),
                   pl.BlockSpec((nb, SUBLANES, r), per_b),
                   pl.BlockSpec((nb, 1, r), per_b)],
        out_shape=[jax.ShapeDtypeStruct((b, s, d), F32),
                   jax.ShapeDtypeStruct((b, SUBLANES, r), F32),
                   jax.ShapeDtypeStruct((b, 1, r), F32)],
        scratch_shapes=[pltpu.VMEM((nb, SUBLANES, r), F32), pltpu.VMEM((nb, 1, r), F32)],
        compiler_params=_params(2),
        name="lru_layer",
    )(x, mod, row(norm_g), w_in, conv_w, row(conv_b), wg, row(ba), row(bx), row(lam), w_out, cs0, h0)


def _attproj_body(x_ref, mod_ref, g_ref, w_ref, *refs, att_scale):
    p0_ref, p1_ref, p2_ref, gate_ref, t0_ref, t1_ref, t2_ref, stage_sc = refs[-8:]
    _, t, d = x_ref.shape
    mod = mod_ref[...]
    hn = _adaln_norm(x_ref[...], g_ref[...], mod[..., :d], mod[..., d:2 * d])
    hn = hn.reshape(t, d).astype(BF16)
    tails = (t0_ref, t1_ref, t2_ref)
    qkvs = (p0_ref, p1_ref, p2_ref)
    for c in range(3 * N_GROUPS + 1):
        res = jnp.dot(hn, w_ref[:, c * N_CHUNK:(c + 1) * N_CHUNK], preferred_element_type=F32)
        g, j = divmod(c, 3)
        if g == N_GROUPS:
            gate_ref[...] = res.astype(gate_ref.dtype)
            continue
        if j == 0:
            res = res * att_scale
        else:
            rows = tails[g].shape[0]
            tails[g][:, (j - 1) * N_CHUNK:j * N_CHUNK] = res[t - rows:, :]
        cols = slice(j * N_CHUNK, (j + 1) * N_CHUNK)
        dil = qkvs[g].shape[0]
        if dil == 1:
            qkvs[g][0, :, cols] = res.astype(qkvs[g].dtype)
        else:
            nblk = N_CHUNK // LANES
            for cb in range(nblk):
                stage_sc[cb] = res[:, cb * LANES:(cb + 1) * LANES]
            for r in range(dil):
                picked = [stage_sc[cb, pl.ds(r, t // dil, stride=dil), :] for cb in range(nblk)]
                qkvs[g][r, :, cols] = jnp.concatenate(picked, axis=1).astype(qkvs[g].dtype)


def _att_proj(x, mod, norm_g, w_in, layer, t, dils, tail_lens, p_dtype, tails_so_far=None):
    b, s, d = x.shape
    n_layers = w_in.shape[0]
    n_t = s // t
    qkv_shapes = [jax.ShapeDtypeStruct((b, dil, s // dil, 3 * N_CHUNK), p_dtype) for dil in dils]
    qkv_specs = [pl.BlockSpec((None, dil, t // dil, 3 * N_CHUNK), lambda i, j: (i, 0, j, 0)) for dil in dils]
    tail_shapes, tail_specs = [], []
    for g in range(N_GROUPS):
        wlen = tail_lens[g]
        rows = min(wlen, t)
        first = n_t - wlen // rows
        tail_shapes.append(jax.ShapeDtypeStruct((n_layers, b, wlen, 2 * N_CHUNK), F32))
        tail_specs.append(pl.BlockSpec((None, None, rows, 2 * N_CHUNK),
                                       lambda i, j, first=first: (layer, i, jnp.maximum(j - first, 0), 0)))
    head_dim = d // N_HEADS
    inputs = [x, mod, norm_g.reshape(1, -1), w_in]
    in_specs = [pl.BlockSpec((1, t, d), lambda i, j: (i, j, 0)),
                _mod_spec(mod, 1, t),
                _const_spec((1, d)),
                _layer_spec(w_in, layer)]
    aliases = {}
    if tails_so_far is not None:
        n_out_before_tails = len(dils) + 1
        for g, tail in enumerate(tails_so_far):
            aliases[len(inputs)] = n_out_before_tails + g
            inputs.append(tail)
            in_specs.append(pl.BlockSpec(memory_space=pl.ANY))
    return pl.pallas_call(
        functools.partial(_attproj_body, att_scale=head_dim ** -0.5),
        grid=(b, n_t),
        in_specs=in_specs,
        out_specs=qkv_specs + [pl.BlockSpec((None, t, N_CHUNK), lambda i, j: (i, j, 0))] + tail_specs,
        out_shape=qkv_shapes + [jax.ShapeDtypeStruct((b, s, N_CHUNK), p_dtype)] + tail_shapes,
        scratch_shapes=[pltpu.VMEM((N_CHUNK // LANES, t, LANES), F32)],
        input_output_aliases=aliases,
        compiler_params=_params(2),
        name="att_proj",
    )(*inputs)


def _slope(h):
    return 2.0 ** (-8.0 * (h + 1) / N_HEADS)


def _stat_lane(h):
    return h


def _pattn_body(q_ref, kh_ref, kc_ref, vh_ref, vc_ref, o_ref, m_ref, l_ref, k_sc, v_sc, pen_sc, *, dil, n):
    rb, qb, _ = q_ref.shape

    @pl.when((pl.program_id(0) == 0) & (pl.program_id(1) == 0) & (pl.program_id(2) == 0))
    def _():
        ii = lax.broadcasted_iota(jnp.int32, (n, 2 * n), 0)
        jj = lax.broadcasted_iota(jnp.int32, (n, 2 * n), 1)
        diff = ii - jj + n
        dist = jnp.where((diff >= 0) & (diff <= n), (diff * dil).astype(F32), MASK_DIST)
        dist_first = jnp.where(jj >= n, dist, MASK_DIST)
        for h in range(N_HEADS):
            pen_sc[0, h] = _slope(h) * dist
            pen_sc[1, h] = _slope(h) * dist_first

    k_sc[:, :n, :] = kh_ref[...]
    k_sc[:, n:, :] = kc_ref[...]
    v_sc[:, :n, :] = vh_ref[...]
    v_sc[:, n:, :] = vc_ref[...]
    lane = lax.broadcasted_iota(jnp.int32, (n, LANES), 1)
    low = lane < (LANES // 2)
    ones = jnp.ones((2 * n, LANES), BF16)
    nt = (((1,), (1,)), ((), ()))
    for ri, sc in [(ri, sc) for ri in range(rb) for sc in range(qb // n)]:
        r0 = sc * n
        table = jnp.where(pl.program_id(2) > 0, 0, 1) if sc == 0 else 0
        m_all = jnp.zeros((n, LANES), F32)
        l_all = jnp.ones((n, LANES), F32)
        for p in range(N_HEADS // 2):
            cols = slice(p * LANES, (p + 1) * LANES)
            q2 = q_ref[ri, r0:r0 + n, cols]
            k2 = k_sc[ri, r0:r0 + 2 * n, cols]
            v2 = jnp.concatenate([v_sc[ri, r0:r0 + 2 * n, cols], ones], axis=1)
            outs = []
            for half in range(2):
                h = 2 * p + half
                qm = jnp.where(low if half == 0 else ~low, q2, jnp.zeros_like(q2))
                s = lax.dot_general(qm, k2, nt, preferred_element_type=F32) - pen_sc[table, h]
                m = jnp.max(s, axis=-1, keepdims=True)
                e = jnp.exp(s - m).astype(BF16)
                ov = jnp.dot(e, v2, preferred_element_type=F32)
                outs.append(ov[:, :LANES])
                m_all = jnp.where(lane == h, m, m_all)
                l_all = jnp.where(lane == h, ov[:, LANES:], l_all)
            o_ref[ri, r0:r0 + n, cols] = jnp.where(low, outs[0], outs[1]).astype(o_ref.dtype)
        m_ref[ri, r0:r0 + n, :] = m_all
        l_ref[ri, r0:r0 + n, :] = l_all


def _prompt_attn(qkv, g):
    b, dil, steps, _ = qkv.shape
    n = WINDOWS[g] // dil
    qb = min(steps, ATTN_QUERY_BLOCK)
    rb = min(dil, ATTN_QUERY_BLOCK // qb)
    cur = lambda col: (lambda i, r, c: (i, r, c, col))
    halo = lambda col: (lambda i, r, c: (i, r, jnp.maximum(c * (qb // n) - 1, 0), col))
    cur_spec = lambda col: pl.BlockSpec((None, rb, qb, N_CHUNK), cur(col))
    halo_spec = lambda col: pl.BlockSpec((None, rb, n, N_CHUNK), halo(col))
    return pl.pallas_call(
        functools.partial(_pattn_body, dil=dil, n=n),
        grid=(b, dil // rb, steps // qb),
        in_specs=[cur_spec(0), halo_spec(1), cur_spec(1), halo_spec(2), cur_spec(2)],
        out_specs=[pl.BlockSpec((None, rb, qb, N_CHUNK), cur(0)),
                   pl.BlockSpec((None, rb, qb, LANES), cur(0)),
                   pl.BlockSpec((None, rb, qb, LANES), cur(0))],
        out_shape=[jax.ShapeDtypeStruct((b, dil, steps, N_CHUNK), BF16),
                   jax.ShapeDtypeStruct((b, dil, steps, LANES), F32),
                   jax.ShapeDtypeStruct((b, dil, steps, LANES), F32)],
        scratch_shapes=[pltpu.VMEM((rb, qb + n, N_CHUNK), BF16), pltpu.VMEM((rb, qb + n, N_CHUNK), BF16),
                        pltpu.VMEM((2, N_HEADS, n, 2 * n), F32)],
        compiler_params=_params(3),
        name=f"prompt_attn_g{g}",
    )(qkv, qkv, qkv, qkv, qkv)


def _sattn_body(p_ref, cache_ref, o_ref, m_ref, l_ref, m_sc, l_sc, acc_sc, *, dil, window):
    nb, t, _ = p_ref.shape
    pc = cache_ref.shape[2]
    hd = N_CHUNK // N_HEADS
    rows = N_HEADS * t
    j = pl.program_id(1)
    buf_len = pc * pl.num_programs(1)
    nt = (((1,), (1,)), ((), ()))

    own_head = (lax.broadcasted_iota(jnp.int32, (rows, N_CHUNK), 0) // t
                == lax.broadcasted_iota(jnp.int32, (rows, N_CHUNK), 1) // hd)
    ri = lax.broadcasted_iota(jnp.int32, (rows, 1), 0)
    head = ri // t
    tq = ri - head * t
    slope = jnp.exp((head + 1).astype(F32) * (-8.0 / N_HEADS * math.log(2.0)))

    def block_diag_q(bi):
        q = p_ref[bi, :, :N_CHUNK]
        qbd = jnp.where(own_head, jnp.broadcast_to(q[None], (N_HEADS, t, N_CHUNK)).reshape(rows, N_CHUNK), 0.0)
        return qbd.astype(BF16)

    qbds = [block_diag_q(bi) for bi in range(nb)]

    @pl.when(j == 0)
    def _():
        pad = jnp.zeros((LANES - t, N_CHUNK), F32)
        dist = tq - lax.broadcasted_iota(jnp.int32, (rows, LANES), 1)
        ok = (dist >= 0) & ((dist & (dil - 1)) == 0)
        pen = slope * dist.astype(F32)
        for bi in range(nb):
            kn = jnp.concatenate([p_ref[bi, :, N_CHUNK:2 * N_CHUNK], pad], axis=0).astype(BF16)
            vn = jnp.concatenate([p_ref[bi, :, 2 * N_CHUNK:], pad], axis=0).astype(BF16)
            s = lax.dot_general(qbds[bi], kn, nt, preferred_element_type=F32)
            s = jnp.where(ok, s - pen, NEG_INF)
            m = jnp.max(s, axis=-1, keepdims=True)
            e = jnp.exp(s - m)
            m_sc[bi] = m
            l_sc[bi] = jnp.sum(e, axis=-1, keepdims=True)
            acc_sc[bi] = jnp.dot(e.astype(BF16), vn, preferred_element_type=F32)

    pos = j * pc + lax.broadcasted_iota(jnp.int32, (rows, pc), 1)
    dist = buf_len + tq - pos
    ok = ((dist & (dil - 1)) == 0) & (dist <= window)
    pen = slope * dist.astype(F32)
    for bi in range(nb):
        kt = cache_ref[bi, :N_CHUNK, :].astype(BF16)
        vt = cache_ref[bi, N_CHUNK:, :].astype(BF16)
        s = jnp.dot(qbds[bi], kt, preferred_element_type=F32)
        s = jnp.where(ok, s - pen, NEG_INF)
        m_old = m_sc[bi]
        m_new = jnp.maximum(m_old, jnp.max(s, axis=-1, keepdims=True))
        alpha = jnp.exp(m_old - m_new)
        e = jnp.exp(s - m_new)
        l_sc[bi] = alpha * l_sc[bi] + jnp.sum(e, axis=-1, keepdims=True)
        acc_sc[bi] = alpha * acc_sc[bi] + lax.dot_general(e.astype(BF16), vt, nt, preferred_element_type=F32)
        m_sc[bi] = m_new

    @pl.when(j == pl.num_programs(1) - 1)
    def _():
        on_lane = (_stat_lane(lax.broadcasted_iota(jnp.int32, (rows, LANES), 0) // t)
                   == lax.broadcasted_iota(jnp.int32, (rows, LANES), 1))
        fold = lambda v: jnp.sum(v.reshape(N_HEADS, t, v.shape[-1]), axis=0)
        used = fold(on_lane.astype(F32))
        for bi in range(nb):
            o_ref[bi] = fold(jnp.where(own_head, acc_sc[bi], 0.0))
            m_ref[bi] = fold(jnp.where(on_lane, m_sc[bi], 0.0))
            l_ref[bi] = fold(jnp.where(on_lane, l_sc[bi], 0.0)) + (1.0 - used)


def _sample_attn(qkv, cache, layer, g):
    b, t, _ = qkv.shape
    natt, _, blen = cache.shape[:3]
    cache_t = jnp.transpose(cache, (0, 1, 3, 4, 5, 2)).reshape(natt, b, 2 * N_CHUNK, blen)
    pc = min(blen, SAMPLE_POS_CHUNK)
    nb = max(1, min(b, SAMPLE_CACHE_BLOCK_BYTES // (2 * N_CHUNK * pc * 4)))
    per_b = lambda i, j: (i, 0, 0)
    return pl.pallas_call(
        functools.partial(_sattn_body, dil=DILATIONS[g], window=WINDOWS[g]),
        grid=(b // nb, blen // pc),
        in_specs=[pl.BlockSpec((nb, t, 3 * N_CHUNK), per_b),
                  pl.BlockSpec((None, nb, 2 * N_CHUNK, pc), lambda i, j: (layer, i, 0, j))],
        out_specs=[pl.BlockSpec((nb, t, N_CHUNK), per_b),
                   pl.BlockSpec((nb, t, LANES), per_b),
                   pl.BlockSpec((nb, t, LANES), per_b)],
        out_shape=[jax.ShapeDtypeStruct((b, t, N_CHUNK), F32),
                   jax.ShapeDtypeStruct((b, t, LANES), F32),
                   jax.ShapeDtypeStruct((b, t, LANES), F32)],
        scratch_shapes=[pltpu.VMEM((nb, N_HEADS * t, 1), F32), pltpu.VMEM((nb, N_HEADS * t, 1), F32),
                        pltpu.VMEM((nb, N_HEADS * t, N_CHUNK), F32)],
        compiler_params=_params(2),
        name=f"sample_attn_g{g}",
    )(qkv, cache_t)


def _natural_rows(ref, stage_sc):
    dil, n, c = ref.shape
    if dil == 1:
        return ref[0].astype(F32)
    nblk = c // LANES
    for r in range(dil):
        rows = ref[r].astype(F32)
        for cb in range(nblk):
            stage_sc[cb, pl.ds(r, n, stride=dil), :] = rows[:, cb * LANES:(cb + 1) * LANES]
    return jnp.concatenate([stage_sc[cb] for cb in range(nblk)], axis=1)


def _attout_body(x_ref, mod_ref, gate_ref, o0_ref, o1_ref, o2_ref, m0_ref, m1_ref, m2_ref,
                 l0_ref, l1_ref, l2_ref, wout_ref, fg_ref, y_ref, stage_sc, *, final_norm):
    _, t, d = x_ref.shape
    hd = d // N_HEADS
    ms = [_natural_rows(r, stage_sc) for r in (m0_ref, m1_ref, m2_ref)]
    ls = [_natural_rows(r, stage_sc) for r in (l0_ref, l1_ref, l2_ref)]
    m = jnp.maximum(jnp.maximum(ms[0], ms[1]), ms[2])
    es = [jnp.exp(mg - m) for mg in ms]
    den = es[0] * ls[0] + es[1] * ls[1] + es[2] * ls[2]
    expand = (lax.broadcasted_iota(jnp.int32, (2 * LANES, d), 0) % LANES
              == _stat_lane(lax.broadcasted_iota(jnp.int32, (2 * LANES, d), 1) // hd)).astype(BF16)
    acc = jnp.zeros((t, d), F32)
    for e, o_ref in zip(es, (o0_ref, o1_ref, o2_ref)):
        w = e / den
        hi = w.astype(BF16)
        lo = (w - hi.astype(F32)).astype(BF16)
        wfull = jnp.dot(jnp.concatenate([hi, lo], axis=1), expand, preferred_element_type=F32)
        acc = acc + wfull * _natural_rows(o_ref, stage_sc)
    y = (acc * _silu(gate_ref[...].astype(F32))).astype(BF16)
    out = jnp.dot(y, wout_ref[...], preferred_element_type=F32)
    xn = x_ref[0] + mod_ref[0][:, 2 * d:] * out
    if final_norm:
        xn = xn * lax.rsqrt(jnp.mean(xn * xn, axis=-1, keepdims=True) + EPS) * fg_ref[...]
    y_ref[0] = xn


def _att_out(x, mod, gate, os_, ms, ls, w_out, layer, final_g, final_norm, t):
    b, s, d = x.shape
    blk = lambda i, j: (i, j, 0)
    res_spec = lambda v: pl.BlockSpec((None, v.shape[1], t // v.shape[1], v.shape[3]), lambda i, j: (i, 0, j, 0))
    return pl.pallas_call(
        functools.partial(_attout_body, final_norm=final_norm),
        grid=(b, s // t),
        in_specs=[pl.BlockSpec((1, t, d), blk),
                  _mod_spec(mod, 1, t),
                  pl.BlockSpec((None, t, N_CHUNK), blk)]
                 + [res_spec(v) for v in (*os_, *ms, *ls)]
                 + [_layer_spec(w_out, layer), _const_spec((1, d))],
        out_specs=pl.BlockSpec((1, t, d), blk),
        out_shape=jax.ShapeDtypeStruct((b, s, d), F32),
        scratch_shapes=[pltpu.VMEM((N_CHUNK // LANES, t, LANES), F32)],
        compiler_params=_params(2),
        name="att_out",
    )(x, mod, gate, *os_, *ms, *ls, w_out, final_g.reshape(1, -1))


def kernel(x_prompt, x_sample, state_conv, state_h, cache_kv_g0, cache_kv_g1, cache_kv_g2, c_prompt, c_sample, norm_g, ada_w, ada_b, final_g, lru_w_in, lru_conv_w, lru_conv_b, lru_wa, lru_ba, lru_wx, lru_bx, lru_lambda, lru_w_out, att_w_in, att_w_out):
    bp, sp, d = x_prompt.shape
    bs, ts, _ = x_sample.shape
    r = lru_conv_w.shape[-1]
    hd = d // N_HEADS
    caches = (cache_kv_g0, cache_kv_g1, cache_kv_g2)

    n_c = bp + bs
    c_all = jnp.concatenate([c_prompt, c_sample, jnp.zeros((-n_c % 16, d), F32)], axis=0)
    mod = _adaln_mod(c_all, ada_w, ada_b)
    mod_p = mod[:, :bp].reshape(DEPTH, bp, 1, 3 * d)
    mod_s = mod[:, bp:n_c].reshape(DEPTH, bs, 1, 3 * d)

    lru_w_in, lru_w_out, att_w_in, att_w_out = (w.astype(BF16) for w in (lru_w_in, lru_w_out, att_w_in, att_w_out))
    yp, ys = x_prompt, x_sample
    p_conv, p_h, s_conv, s_h = [], [], [], []
    p_kv = s_kv = None
    n_keep = CONV_W - 1
    for i in range(DEPTH):
        j = i // N_MIXERS
        if i % N_MIXERS == 0:
            wg = _gate_weights(lru_wa[j], lru_wx[j])
            args = (j, norm_g[i], lru_w_in, lru_conv_w[j], lru_conv_b[j], wg, lru_ba[j],
                    lru_bx[j], lru_lambda[j], lru_w_out)
            zc = jnp.zeros((bp, SUBLANES, r), F32)
            zh = jnp.zeros((bp, 1, r), F32)
            yp, cs, ht = _lru_layer(yp, mod_p[i], *args, zc, zh, 1, LRU_TILE)
            p_conv.append(cs[:, SUBLANES - n_keep:])
            p_h.append(ht[:, 0])
            cs0 = jnp.pad(state_conv[j], ((0, 0), (SUBLANES - n_keep, 0), (0, 0)))
            ys, cs, ht = _lru_layer(ys, mod_s[i], *args, cs0, state_h[j][:, None, :], bs, ts)
            s_conv.append(cs[:, SUBLANES - n_keep:])
            s_h.append(ht[:, 0])
        else:
            last = i == DEPTH - 1
            p_tails = [min(w, sp) for w in WINDOWS]
            *qkvs, gate, t0, t1, t2 = _att_proj(yp, mod_p[i], norm_g[i], att_w_in, j, PROMPT_TILE, DILATIONS,
                                                p_tails, BF16, p_kv)
            p_kv = (t0, t1, t2)
            res = [_prompt_attn(qkvs[g], g) for g in range(N_GROUPS)]
            yp = _att_out(yp, mod_p[i], gate, *zip(*res), att_w_out, j, final_g, last, ATT_OUT_TILE)
            n_s = bs * ts
            flat = lambda v: v.reshape(1, n_s, v.shape[-1])
            mod_rows = flat(jnp.broadcast_to(mod_s[i], (bs, ts, 3 * d)))
            *qkvs, gate, t0, t1, t2 = _att_proj(flat(ys), mod_rows, norm_g[i], att_w_in, j, n_s,
                                                (1,) * N_GROUPS, [n_s] * N_GROUPS, F32, s_kv)
            s_kv = (t0, t1, t2)
            res = [_sample_attn(qkvs[g].reshape(bs, ts, -1), caches[g], j, g) for g in range(N_GROUPS)]
            nat = lambda v: v.reshape(1, 1, n_s, v.shape[-1])
            res = [[nat(v) for v in group] for group in res]
            ys = _att_out(flat(ys), mod_rows, gate, *zip(*res), att_w_out, j, final_g, last, n_s).reshape(bs, ts, d)
    natt = att_w_in.shape[0]
    p_kv = [tail.reshape(natt, bp, -1, 2, N_HEADS, hd) for tail in p_kv]
    s_kv = [tail.reshape(natt, bs, ts, 2, N_HEADS, hd) for tail in s_kv]
    return (yp, ys, jnp.stack(p_conv), jnp.stack(p_h), *p_kv, jnp.stack(s_conv), jnp.stack(s_h), *s_kv)
```

```python
import functools
import math

import jax
import jax.numpy as jnp
from jax import lax
from jax.experimental import pallas as pl
from jax.experimental.pallas import tpu as pltpu

F32 = jnp.float32
BF16 = jnp.bfloat16

DEPTH = 4
N_MIXERS = 2
N_BLOCKS = 16
CONV_W = 4
LRU_C = 8.0
N_HEADS = 16
WINDOWS = (128, 512, 2048)
DILATIONS = (1, 4, 16)
N_GROUPS = 3
NEG_INF = -1e30
EPS = 1e-6

LANES = 128
SUBLANES = 8
VMEM_LIMIT = 56 * 1024 * 1024
PROMPT_TILE = 256
LRU_TILE = 512
ATT_OUT_TILE = 512
GATE_GROUP = 4
N_CHUNK = 1024
SAMPLE_POS_CHUNK = 2048
SAMPLE_CACHE_BLOCK_BYTES = 8 * 1024 * 1024
ATTN_QUERY_BLOCK = 1024
MASK_DIST = 1e33


def _silu(x):
    return x * jax.nn.sigmoid(x)


def _adaln_norm(x, g, shift, scale):
    y = x * lax.rsqrt(jnp.mean(x * x, axis=-1, keepdims=True) + EPS)
    return (y * g) * (1.0 + scale) + shift


def _const_spec(shape):
    nd = len(shape)
    return pl.BlockSpec(shape, lambda *_: (0,) * nd, pipeline_mode=pl.Buffered(1))


def _layer_spec(stacked, layer):
    shape = stacked.shape[1:]
    nd = len(shape)
    return pl.BlockSpec((None,) + shape, lambda *_: (layer,) + (0,) * nd, pipeline_mode=pl.Buffered(1))


def _params(n_grid):
    return pltpu.CompilerParams(dimension_semantics=("arbitrary",) * n_grid,
                                vmem_limit_bytes=VMEM_LIMIT)


def _mod_spec(mod, nb, t):
    _, mt, d3 = mod.shape
    if mt == 1:
        return pl.BlockSpec((nb, 1, d3), lambda i, j: (i, 0, 0))
    return pl.BlockSpec((nb, t, d3), lambda i, j: (i, j, 0))


def _mod_body(c_ref, w_ref, b_ref, o_ref):
    a = _silu(c_ref[...]).astype(BF16)
    o_ref[...] = jnp.dot(a, w_ref[...].astype(BF16), preferred_element_type=F32) + b_ref[...]


def _adaln_mod(c_all, ada_w, ada_b):
    r, d = c_all.shape
    depth, _, d3 = ada_w.shape
    tn = N_CHUNK
    return pl.pallas_call(
        _mod_body,
        grid=(depth, d3 // tn),
        in_specs=[pl.BlockSpec((r, d), lambda i, n: (0, 0)),
                  pl.BlockSpec((None, d, tn), lambda i, n: (i, 0, n)),
                  pl.BlockSpec((None, 1, tn), lambda i, n: (i, 0, n))],
        out_specs=pl.BlockSpec((None, r, tn), lambda i, n: (i, 0, n)),
        out_shape=jax.ShapeDtypeStruct((depth, r, d3), F32),
        compiler_params=_params(2),
        name="adaln_mod",
    )(c_all, ada_w, ada_b.reshape(depth, 1, d3))


def _lru_body(x_ref, mod_ref, g_ref, win_ref, cw_ref, cb_ref, wg_ref, ba_ref, bx_ref, lam_ref,
              wout_ref, cs0_ref, h0_ref, *refs):
    y_ref, cs_ref, ht_ref, prev_sc, h_sc = refs[-5:]
    nb, t, d = x_ref.shape
    taps, r = cw_ref.shape
    assert taps == 4, "the conv below pairs its taps two by two"
    gw = wg_ref.shape[1]

    @pl.when(pl.program_id(1) == 0)
    def _():
        prev_sc[...] = cs0_ref[...]
        h_sc[...] = h0_ref[...]

    x = x_ref[...]
    mod = mod_ref[...]
    shift, scale, gate = mod[..., :d], mod[..., d:2 * d], mod[..., 2 * d:]
    hn = _adaln_norm(x, g_ref[...], shift, scale).reshape(nb * t, d).astype(BF16)
    proj = jnp.dot(hn, win_ref[...], preferred_element_type=F32)
    xb = proj[:, :r].reshape(nb, t, r)
    gb = proj[:, r:]

    cat = jnp.concatenate([prev_sc[...], xb], axis=1)
    cw = cw_ref[...]
    back1 = pltpu.roll(cat, 1, 1)
    older = pltpu.roll(cat * cw[1:2] + back1 * cw[0:1], 2, 1)
    xc = cb_ref[...] + xb * cw[3:4] + back1[:, SUBLANES:, :] * cw[2:3] + older[:, SUBLANES:, :]
    new_prev = cat[:, t:, :]
    prev_sc[...] = new_prev
    cs_ref[...] = new_prev

    xc2 = xc.reshape(nb * t, r)
    xcb = xc2.astype(BF16)
    grs, gis = [], []
    for q in range(r // gw):
        p = jnp.dot(xcb[:, q * gw:(q + 1) * gw], wg_ref[q], preferred_element_type=F32)
        grs.append(p[:, :gw])
        gis.append(p[:, gw:])
    rg = jax.nn.sigmoid(jnp.concatenate(grs, axis=1) + ba_ref[...])
    ig = jax.nn.sigmoid(jnp.concatenate(gis, axis=1) + bx_ref[...])
    nlam = -lam_ref[...]
    softplus = jnp.maximum(nlam, 0.0) + jnp.log1p(jnp.exp(-jnp.abs(nlam)))
    z = LRU_C * rg * softplus
    a = jnp.exp(-z)
    u = jnp.sqrt(jnp.tanh(z) * (1.0 + a * a)) * ig * xc2

    groups = t // SUBLANES
    a3 = a.reshape(nb * groups, SUBLANES, r)
    u3 = u.reshape(nb * groups, SUBLANES, r)
    row = lax.broadcasted_iota(jnp.int32, (1, SUBLANES, r), 1)
    dist = 1
    while dist < SUBLANES:
        keep = row >= dist
        u3 = jnp.where(keep, a3 * pltpu.roll(u3, dist, 1) + u3, u3)
        a3 = jnp.where(keep, a3 * pltpu.roll(a3, dist, 1), a3)
        dist *= 2
    a4 = a3.reshape(nb, groups, SUBLANES, r)
    u4 = u3.reshape(nb, groups, SUBLANES, r)
    h_prev = h_sc[...]
    hs = []
    for gi in range(groups):
        hs.append(a4[:, gi] * h_prev + u4[:, gi])
        h_prev = hs[-1][:, SUBLANES - 1:, :]
    h = hs[0] if groups == 1 else jnp.concatenate(hs, axis=1)
    h_last = h_prev
    h_sc[...] = h_last
    ht_ref[...] = h_last

    y = (h.reshape(nb * t, r) * _silu(gb)).astype(BF16)
    out = jnp.dot(y, wout_ref[...], preferred_element_type=F32).reshape(nb, t, d)
    y_ref[...] = x + gate * out


def _gate_weights(wa, wx):
    nblk, bw, _ = wa.shape
    ngrp = nblk // GATE_GROUP
    eye = jnp.eye(GATE_GROUP, dtype=wa.dtype)[None, :, None, :, None]

    def block_diag(w):
        w = w.reshape(ngrp, GATE_GROUP, bw, 1, bw) * eye
        return w.reshape(ngrp, GATE_GROUP * bw, GATE_GROUP * bw)

    return jnp.concatenate([block_diag(wa), block_diag(wx)], axis=-1).astype(BF16)


def _lru_layer(x, mod, layer, norm_g, w_in, conv_w, conv_b, wg, ba, bx, lam, w_out, cs0, h0, nb, t,
               states_so_far=None):
    b, s, d = x.shape
    n_layers = w_in.shape[0]
    r = conv_w.shape[1]
    gw = wg.shape[1]
    row = lambda v: v.reshape(1, -1)
    blk = lambda i, j: (i, j, 0)
    per_b = lambda i, j: (i, 0, 0)
    state = lambda i, j: (layer, i, 0, 0)
    inputs = [x, mod, row(norm_g), w_in, conv_w, row(conv_b), wg, row(ba), row(bx), row(lam), w_out, cs0, h0]
    in_specs = [pl.BlockSpec((nb, t, d), blk),
                pl.BlockSpec((nb, 1, 3 * d), per_b),
                _const_spec((1, d)),
                _layer_spec(w_in, layer),
                _const_spec((CONV_W, r)),
                _const_spec((1, r)),
                _const_spec((r // gw, gw, 2 * gw)),
                _const_spec((1, r)),
                _const_spec((1, r)),
                _const_spec((1, r)),
                _layer_spec(w_out, layer),
                pl.BlockSpec((nb, SUBLANES, r), per_b),
                pl.BlockSpec((nb, 1, r), per_b)]
    aliases = {}
    if states_so_far is not None:
        for k, arr in enumerate(states_so_far):
            aliases[len(inputs)] = 1 + k
            inputs.append(arr)
            in_specs.append(pl.BlockSpec(memory_space=pl.ANY))
    return pl.pallas_call(
        _lru_body,
        grid=(b // nb, s // t),
        in_specs=in_specs,
        out_specs=[pl.BlockSpec((nb, t, d), blk),
                   pl.BlockSpec((None, nb, SUBLANES, r), state),
                   pl.BlockSpec((None, nb, 1, r), state)],
        out_shape=[jax.ShapeDtypeStruct((b, s, d), F32),
                   jax.ShapeDtypeStruct((n_layers, b, SUBLANES, r), F32),
                   jax.ShapeDtypeStruct((n_layers, b, 1, r), F32)],
        scratch_shapes=[pltpu.VMEM((nb, SUBLANES, r), F32), pltpu.VMEM((nb, 1, r), F32)],
        input_output_aliases=aliases,
        compiler_params=_params(2),
        name="lru_layer",
    )(*inputs)


def _attproj_body(x_ref, mod_ref, g_ref, w_ref, *refs, att_scale):
    p0_ref, p1_ref, p2_ref, gate_ref, t0_ref, t1_ref, t2_ref, stage_sc = refs[-8:]
    _, t, d = x_ref.shape
    mod = mod_ref[...]
    hn = _adaln_norm(x_ref[...], g_ref[...], mod[..., :d], mod[..., d:2 * d])
    hn = hn.reshape(t, d).astype(BF16)
    tails = (t0_ref, t1_ref, t2_ref)
    qkvs = (p0_ref, p1_ref, p2_ref)
    for c in range(3 * N_GROUPS + 1):
        res = jnp.dot(hn, w_ref[:, c * N_CHUNK:(c + 1) * N_CHUNK], preferred_element_type=F32)
        g, j = divmod(c, 3)
        if g == N_GROUPS:
            gate_ref[...] = res.astype(gate_ref.dtype)
            continue
        if j == 0:
            res = res * att_scale
        else:
            rows = tails[g].shape[0]
            tails[g][:, (j - 1) * N_CHUNK:j * N_CHUNK] = res[t - rows:, :]
        cols = slice(j * N_CHUNK, (j + 1) * N_CHUNK)
        dil = qkvs[g].shape[0]
        if dil == 1:
            qkvs[g][0, :, cols] = res.astype(qkvs[g].dtype)
        else:
            nblk = N_CHUNK // LANES
            for cb in range(nblk):
                stage_sc[cb] = res[:, cb * LANES:(cb + 1) * LANES]
            for r in range(dil):
                picked = [stage_sc[cb, pl.ds(r, t // dil, stride=dil), :] for cb in range(nblk)]
                qkvs[g][r, :, cols] = jnp.concatenate(picked, axis=1).astype(qkvs[g].dtype)


def _att_proj(x, mod, norm_g, w_in, layer, t, dils, tail_lens, p_dtype, tails_so_far=None):
    b, s, d = x.shape
    n_layers = w_in.shape[0]
    n_t = s // t
    qkv_shapes = [jax.ShapeDtypeStruct((b, dil, s // dil, 3 * N_CHUNK), p_dtype) for dil in dils]
    qkv_specs = [pl.BlockSpec((None, dil, t // dil, 3 * N_CHUNK), lambda i, j: (i, 0, j, 0)) for dil in dils]
    tail_shapes, tail_specs = [], []
    for g in range(N_GROUPS):
        wlen = tail_lens[g]
        rows = min(wlen, t)
        first = n_t - wlen // rows
        tail_shapes.append(jax.ShapeDtypeStruct((n_layers, b, wlen, 2 * N_CHUNK), F32))
        tail_specs.append(pl.BlockSpec((None, None, rows, 2 * N_CHUNK),
                                       lambda i, j, first=first: (layer, i, jnp.maximum(j - first, 0), 0)))
    head_dim = d // N_HEADS
    inputs = [x, mod, norm_g.reshape(1, -1), w_in]
    in_specs = [pl.BlockSpec((1, t, d), lambda i, j: (i, j, 0)),
                _mod_spec(mod, 1, t),
                _const_spec((1, d)),
                _layer_spec(w_in, layer)]
    aliases = {}
    if tails_so_far is not None:
        n_out_before_tails = len(dils) + 1
        for g, tail in enumerate(tails_so_far):
            aliases[len(inputs)] = n_out_before_tails + g
            inputs.append(tail)
            in_specs.append(pl.BlockSpec(memory_space=pl.ANY))
    return pl.pallas_call(
        functools.partial(_attproj_body, att_scale=head_dim ** -0.5),
        grid=(b, n_t),
        in_specs=in_specs,
        out_specs=qkv_specs + [pl.BlockSpec((None, t, N_CHUNK), lambda i, j: (i, j, 0))] + tail_specs,
        out_shape=qkv_shapes + [jax.ShapeDtypeStruct((b, s, N_CHUNK), p_dtype)] + tail_shapes,
        scratch_shapes=[pltpu.VMEM((N_CHUNK // LANES, t, LANES), F32)],
        input_output_aliases=aliases,
        compiler_params=_params(2),
        name="att_proj",
    )(*inputs)


def _slope(h):
    return 2.0 ** (-8.0 * (h + 1) / N_HEADS)


def _stat_lane(h):
    return h


def _pattn_body(q_ref, kh_ref, kc_ref, vh_ref, vc_ref, o_ref, m_ref, l_ref, k_sc, v_sc, pen_sc, *, dil, n):
    rb, qb, _ = q_ref.shape

    @pl.when((pl.program_id(0) == 0) & (pl.program_id(1) == 0) & (pl.program_id(2) == 0))
    def _():
        ii = lax.broadcasted_iota(jnp.int32, (n, 2 * n), 0)
        jj = lax.broadcasted_iota(jnp.int32, (n, 2 * n), 1)
        diff = ii - jj + n
        dist = jnp.where((diff >= 0) & (diff <= n), (diff * dil).astype(F32), MASK_DIST)
        dist_first = jnp.where(jj >= n, dist, MASK_DIST)
        for h in range(N_HEADS):
            pen_sc[0, h] = _slope(h) * dist
            pen_sc[1, h] = _slope(h) * dist_first

    k_sc[:, :n, :] = kh_ref[...]
    k_sc[:, n:, :] = kc_ref[...]
    v_sc[:, :n, :] = vh_ref[...]
    v_sc[:, n:, :] = vc_ref[...]
    lane = lax.broadcasted_iota(jnp.int32, (n, LANES), 1)
    low = lane < (LANES // 2)
    ones = jnp.ones((2 * n, LANES), BF16)
    nt = (((1,), (1,)), ((), ()))
    for ri, sc in [(ri, sc) for ri in range(rb) for sc in range(qb // n)]:
        r0 = sc * n
        table = jnp.where(pl.program_id(2) > 0, 0, 1) if sc == 0 else 0
        m_all = jnp.zeros((n, LANES), F32)
        l_all = jnp.ones((n, LANES), F32)
        for p in range(N_HEADS // 2):
            cols = slice(p * LANES, (p + 1) * LANES)
            q2 = q_ref[ri, r0:r0 + n, cols]
            k2 = k_sc[ri, r0:r0 + 2 * n, cols]
            v2 = jnp.concatenate([v_sc[ri, r0:r0 + 2 * n, cols], ones], axis=1)
            outs = []
            for half in range(2):
                h = 2 * p + half
                qm = jnp.where(low if half == 0 else ~low, q2, jnp.zeros_like(q2))
                s = lax.dot_general(qm, k2, nt, preferred_element_type=F32) - pen_sc[table, h]
                m = jnp.max(s, axis=-1, keepdims=True)
                e = jnp.exp(s - m).astype(BF16)
                ov = jnp.dot(e, v2, preferred_element_type=F32)
                outs.append(ov[:, :LANES])
                m_all = jnp.where(lane == h, m, m_all)
                l_all = jnp.where(lane == h, ov[:, LANES:], l_all)
            o_ref[ri, r0:r0 + n, cols] = jnp.where(low, outs[0], outs[1]).astype(o_ref.dtype)
        m_ref[ri, r0:r0 + n, :] = m_all
        l_ref[ri, r0:r0 + n, :] = l_all


def _prompt_attn(qkv, g):
    b, dil, steps, _ = qkv.shape
    n = WINDOWS[g] // dil
    qb = min(steps, ATTN_QUERY_BLOCK)
    rb = min(dil, ATTN_QUERY_BLOCK // qb)
    cur = lambda col: (lambda i, r, c: (i, r, c, col))
    halo = lambda col: (lambda i, r, c: (i, r, jnp.maximum(c * (qb // n) - 1, 0), col))
    cur_spec = lambda col: pl.BlockSpec((None, rb, qb, N_CHUNK), cur(col))
    halo_spec = lambda col: pl.BlockSpec((None, rb, n, N_CHUNK), halo(col))
    return pl.pallas_call(
        functools.partial(_pattn_body, dil=dil, n=n),
        grid=(b, dil // rb, steps // qb),
        in_specs=[cur_spec(0), halo_spec(1), cur_spec(1), halo_spec(2), cur_spec(2)],
        out_specs=[pl.BlockSpec((None, rb, qb, N_CHUNK), cur(0)),
                   pl.BlockSpec((None, rb, qb, LANES), cur(0)),
                   pl.BlockSpec((None, rb, qb, LANES), cur(0))],
        out_shape=[jax.ShapeDtypeStruct((b, dil, steps, N_CHUNK), BF16),
                   jax.ShapeDtypeStruct((b, dil, steps, LANES), F32),
                   jax.ShapeDtypeStruct((b, dil, steps, LANES), F32)],
        scratch_shapes=[pltpu.VMEM((rb, qb + n, N_CHUNK), BF16), pltpu.VMEM((rb, qb + n, N_CHUNK), BF16),
                        pltpu.VMEM((2, N_HEADS, n, 2 * n), F32)],
        compiler_params=_params(3),
        name=f"prompt_attn_g{g}",
    )(qkv, qkv, qkv, qkv, qkv)


def _sattn_body(p_ref, cache_ref, o_ref, m_ref, l_ref, m_sc, l_sc, acc_sc, *, dil, window):
    nb, t, _ = p_ref.shape
    pc = cache_ref.shape[2]
    hd = N_CHUNK // N_HEADS
    rows = N_HEADS * t
    j = pl.program_id(1)
    buf_len = pc * pl.num_programs(1)
    nt = (((1,), (1,)), ((), ()))

    own_head = (lax.broadcasted_iota(jnp.int32, (rows, N_CHUNK), 0) // t
                == lax.broadcasted_iota(jnp.int32, (rows, N_CHUNK), 1) // hd)
    ri = lax.broadcasted_iota(jnp.int32, (rows, 1), 0)
    head = ri // t
    tq = ri - head * t
    slope = jnp.exp((head + 1).astype(F32) * (-8.0 / N_HEADS * math.log(2.0)))

    def block_diag_q(bi):
        q = p_ref[bi, :, :N_CHUNK]
        qbd = jnp.where(own_head, jnp.broadcast_to(q[None], (N_HEADS, t, N_CHUNK)).reshape(rows, N_CHUNK), 0.0)
        return qbd.astype(BF16)

    qbds = [block_diag_q(bi) for bi in range(nb)]

    @pl.when(j == 0)
    def _():
        pad = jnp.zeros((LANES - t, N_CHUNK), F32)
        dist = tq - lax.broadcasted_iota(jnp.int32, (rows, LANES), 1)
        ok = (dist >= 0) & ((dist & (dil - 1)) == 0)
        pen = slope * dist.astype(F32)
        for bi in range(nb):
            kn = jnp.concatenate([p_ref[bi, :, N_CHUNK:2 * N_CHUNK], pad], axis=0).astype(BF16)
            vn = jnp.concatenate([p_ref[bi, :, 2 * N_CHUNK:], pad], axis=0).astype(BF16)
            s = lax.dot_general(qbds[bi], kn, nt, preferred_element_type=F32)
            s = jnp.where(ok, s - pen, NEG_INF)
            m = jnp.max(s, axis=-1, keepdims=True)
            e = jnp.exp(s - m)
            m_sc[bi] = m
            l_sc[bi] = jnp.sum(e, axis=-1, keepdims=True)
            acc_sc[bi] = jnp.dot(e.astype(BF16), vn, preferred_element_type=F32)

    pos = j * pc + lax.broadcasted_iota(jnp.int32, (rows, pc), 1)
    dist = buf_len + tq - pos
    ok = ((dist & (dil - 1)) == 0) & (dist <= window)
    pen = slope * dist.astype(F32)
    for bi in range(nb):
        kt = cache_ref[bi, :N_CHUNK, :].astype(BF16)
        vt = cache_ref[bi, N_CHUNK:, :].astype(BF16)
        s = jnp.dot(qbds[bi], kt, preferred_element_type=F32)
        s = jnp.where(ok, s - pen, NEG_INF)
        m_old = m_sc[bi]
        m_new = jnp.maximum(m_old, jnp.max(s, axis=-1, keepdims=True))
        alpha = jnp.exp(m_old - m_new)
        e = jnp.exp(s - m_new)
        l_sc[bi] = alpha * l_sc[bi] + jnp.sum(e, axis=-1, keepdims=True)
        acc_sc[bi] = alpha * acc_sc[bi] + lax.dot_general(e.astype(BF16), vt, nt, preferred_element_type=F32)
        m_sc[bi] = m_new

    @pl.when(j == pl.num_programs(1) - 1)
    def _():
        on_lane = (_stat_lane(lax.broadcasted_iota(jnp.int32, (rows, LANES), 0) // t)
                   == lax.broadcasted_iota(jnp.int32, (rows, LANES), 1))
        fold = lambda v: jnp.sum(v.reshape(N_HEADS, t, v.shape[-1]), axis=0)
        used = fold(on_lane.astype(F32))
        for bi in range(nb):
            o_ref[bi] = fold(jnp.where(own_head, acc_sc[bi], 0.0))
            m_ref[bi] = fold(jnp.where(on_lane, m_sc[bi], 0.0))
            l_ref[bi] = fold(jnp.where(on_lane, l_sc[bi], 0.0)) + (1.0 - used)


def _sample_attn(qkv, cache, layer, g):
    b, t, _ = qkv.shape
    natt, _, blen = cache.shape[:3]
    cache_t = jnp.transpose(cache, (0, 1, 3, 4, 5, 2)).reshape(natt, b, 2 * N_CHUNK, blen)
    pc = min(blen, SAMPLE_POS_CHUNK)
    nb = max(1, min(b, SAMPLE_CACHE_BLOCK_BYTES // (2 * N_CHUNK * pc * 4)))
    per_b = lambda i, j: (i, 0, 0)
    return pl.pallas_call(
        functools.partial(_sattn_body, dil=DILATIONS[g], window=WINDOWS[g]),
        grid=(b // nb, blen // pc),
        in_specs=[pl.BlockSpec((nb, t, 3 * N_CHUNK), per_b),
                  pl.BlockSpec((None, nb, 2 * N_CHUNK, pc), lambda i, j: (layer, i, 0, j))],
        out_specs=[pl.BlockSpec((nb, t, N_CHUNK), per_b),
                   pl.BlockSpec((nb, t, LANES), per_b),
                   pl.BlockSpec((nb, t, LANES), per_b)],
        out_shape=[jax.ShapeDtypeStruct((b, t, N_CHUNK), F32),
                   jax.ShapeDtypeStruct((b, t, LANES), F32),
                   jax.ShapeDtypeStruct((b, t, LANES), F32)],
        scratch_shapes=[pltpu.VMEM((nb, N_HEADS * t, 1), F32), pltpu.VMEM((nb, N_HEADS * t, 1), F32),
                        pltpu.VMEM((nb, N_HEADS * t, N_CHUNK), F32)],
        compiler_params=_params(2),
        name=f"sample_attn_g{g}",
    )(qkv, cache_t)


def _natural_rows(ref, stage_sc):
    dil, n, c = ref.shape
    if dil == 1:
        return ref[0].astype(F32)
    nblk = c // LANES
    for r in range(dil):
        rows = ref[r].astype(F32)
        for cb in range(nblk):
            stage_sc[cb, pl.ds(r, n, stride=dil), :] = rows[:, cb * LANES:(cb + 1) * LANES]
    return jnp.concatenate([stage_sc[cb] for cb in range(nblk)], axis=1)


def _attout_body(x_ref, mod_ref, gate_ref, o0_ref, o1_ref, o2_ref, m0_ref, m1_ref, m2_ref,
                 l0_ref, l1_ref, l2_ref, wout_ref, fg_ref, y_ref, stage_sc, *, final_norm):
    _, t, d = x_ref.shape
    hd = d // N_HEADS
    ms = [_natural_rows(r, stage_sc) for r in (m0_ref, m1_ref, m2_ref)]
    ls = [_natural_rows(r, stage_sc) for r in (l0_ref, l1_ref, l2_ref)]
    m = jnp.maximum(jnp.maximum(ms[0], ms[1]), ms[2])
    es = [jnp.exp(mg - m) for mg in ms]
    den = es[0] * ls[0] + es[1] * ls[1] + es[2] * ls[2]
    expand = (lax.broadcasted_iota(jnp.int32, (2 * LANES, d), 0) % LANES
              == _stat_lane(lax.broadcasted_iota(jnp.int32, (2 * LANES, d), 1) // hd)).astype(BF16)
    acc = jnp.zeros((t, d), F32)
    for e, o_ref in zip(es, (o0_ref, o1_ref, o2_ref)):
        w = e / den
        hi = w.astype(BF16)
        lo = (w - hi.astype(F32)).astype(BF16)
        wfull = jnp.dot(jnp.concatenate([hi, lo], axis=1), expand, preferred_element_type=F32)
        acc = acc + wfull * _natural_rows(o_ref, stage_sc)
    y = (acc * _silu(gate_ref[...].astype(F32))).astype(BF16)
    out = jnp.dot(y, wout_ref[...], preferred_element_type=F32)
    xn = x_ref[0] + mod_ref[0][:, 2 * d:] * out
    if final_norm:
        xn = xn * lax.rsqrt(jnp.mean(xn * xn, axis=-1, keepdims=True) + EPS) * fg_ref[...]
    y_ref[0] = xn


def _att_out(x, mod, gate, os_, ms, ls, w_out, layer, final_g, final_norm, t):
    b, s, d = x.shape
    blk = lambda i, j: (i, j, 0)
    res_spec = lambda v: pl.BlockSpec((None, v.shape[1], t // v.shape[1], v.shape[3]), lambda i, j: (i, 0, j, 0))
    return pl.pallas_call(
        functools.partial(_attout_body, final_norm=final_norm),
        grid=(b, s // t),
        in_specs=[pl.BlockSpec((1, t, d), blk),
                  _mod_spec(mod, 1, t),
                  pl.BlockSpec((None, t, N_CHUNK), blk)]
                 + [res_spec(v) for v in (*os_, *ms, *ls)]
                 + [_layer_spec(w_out, layer), _const_spec((1, d))],
        out_specs=pl.BlockSpec((1, t, d), blk),
        out_shape=jax.ShapeDtypeStruct((b, s, d), F32),
        scratch_shapes=[pltpu.VMEM((N_CHUNK // LANES, t, LANES), F32)],
        compiler_params=_params(2),
        name="att_out",
    )(x, mod, gate, *os_, *ms, *ls, w_out, final_g.reshape(1, -1))


def kernel(x_prompt, x_sample, state_conv, state_h, cache_kv_g0, cache_kv_g1, cache_kv_g2, c_prompt, c_sample, norm_g, ada_w, ada_b, final_g, lru_w_in, lru_conv_w, lru_conv_b, lru_wa, lru_ba, lru_wx, lru_bx, lru_lambda, lru_w_out, att_w_in, att_w_out):
    bp, sp, d = x_prompt.shape
    bs, ts, _ = x_sample.shape
    r = lru_conv_w.shape[-1]
    hd = d // N_HEADS
    caches = (cache_kv_g0, cache_kv_g1, cache_kv_g2)

    n_c = bp + bs
    c_all = jnp.concatenate([c_prompt, c_sample, jnp.zeros((-n_c % 16, d), F32)], axis=0)
    mod = _adaln_mod(c_all, ada_w, ada_b)
    mod_p = mod[:, :bp].reshape(DEPTH, bp, 1, 3 * d)
    mod_s = mod[:, bp:n_c].reshape(DEPTH, bs, 1, 3 * d)

    lru_w_in, lru_w_out, att_w_in, att_w_out = (w.astype(BF16) for w in (lru_w_in, lru_w_out, att_w_in, att_w_out))
    yp, ys = x_prompt, x_sample
    p_state = s_state = None
    p_kv = s_kv = None
    n_keep = CONV_W - 1
    for i in range(DEPTH):
        j = i // N_MIXERS
        if i % N_MIXERS == 0:
            wg = _gate_weights(lru_wa[j], lru_wx[j])
            args = (j, norm_g[i], lru_w_in, lru_conv_w[j], lru_conv_b[j], wg, lru_ba[j],
                    lru_bx[j], lru_lambda[j], lru_w_out)
            zc = jnp.zeros((bp, SUBLANES, r), F32)
            zh = jnp.zeros((bp, 1, r), F32)
            yp, *p_state = _lru_layer(yp, mod_p[i], *args, zc, zh, 1, LRU_TILE, p_state)
            cs0 = jnp.pad(state_conv[j], ((0, 0), (SUBLANES - n_keep, 0), (0, 0)))
            ys, *s_state = _lru_layer(ys, mod_s[i], *args, cs0, state_h[j][:, None, :], bs, ts, s_state)
        else:
            last = i == DEPTH - 1
            p_tails = [min(w, sp) for w in WINDOWS]
            *qkvs, gate, t0, t1, t2 = _att_proj(yp, mod_p[i], norm_g[i], att_w_in, j, PROMPT_TILE, DILATIONS,
                                                p_tails, BF16, p_kv)
            p_kv = (t0, t1, t2)
            res = [_prompt_attn(qkvs[g], g) for g in range(N_GROUPS)]
            yp = _att_out(yp, mod_p[i], gate, *zip(*res), att_w_out, j, final_g, last, ATT_OUT_TILE)
            n_s = bs * ts
            flat = lambda v: v.reshape(1, n_s, v.shape[-1])
            mod_rows = flat(jnp.broadcast_to(mod_s[i], (bs, ts, 3 * d)))
            *qkvs, gate, t0, t1, t2 = _att_proj(flat(ys), mod_rows, norm_g[i], att_w_in, j, n_s,
                                                (1,) * N_GROUPS, [n_s] * N_GROUPS, F32, s_kv)
            s_kv = (t0, t1, t2)
            res = [_sample_attn(qkvs[g].reshape(bs, ts, -1), caches[g], j, g) for g in range(N_GROUPS)]
            nat = lambda v: v.reshape(1, 1, n_s, v.shape[-1])
            res = [[nat(v) for v in group] for group in res]
            ys = _att_out(flat(ys), mod_rows, gate, *zip(*res), att_w_out, j, final_g, last, n_s).reshape(bs, ts, d)
    natt = att_w_in.shape[0]
    p_kv = [tail.reshape(natt, bp, -1, 2, N_HEADS, hd) for tail in p_kv]
    s_kv = [tail.reshape(natt, bs, ts, 2, N_HEADS, hd) for tail in s_kv]
    (p_conv, p_h), (s_conv, s_h) = ((cs[:, :, SUBLANES - n_keep:], ht[:, :, 0]) for cs, ht in (p_state, s_state))
    return (yp, ys, p_conv, p_h, *p_kv, s_conv, s_h, *s_kv)
```

```python
import functools
import math

import jax
import jax.numpy as jnp
from jax import lax
from jax.experimental import pallas as pl
from jax.experimental.pallas import tpu as pltpu

F32 = jnp.float32
BF16 = jnp.bfloat16

DEPTH = 4
N_MIXERS = 2
N_BLOCKS = 16
CONV_W = 4
LRU_C = 8.0
N_HEADS = 16
WINDOWS = (128, 512, 2048)
DILATIONS = (1, 4, 16)
N_GROUPS = 3
NEG_INF = -1e30
EPS = 1e-6

LANES = 128
SUBLANES = 8
VMEM_LIMIT = 56 * 1024 * 1024
PROMPT_TILE = 256
LRU_TILE = 512
ATT_OUT_TILE = 512
GATE_GROUP = 4
N_CHUNK = 1024
SAMPLE_POS_CHUNK = 2048
SAMPLE_CACHE_BLOCK_BYTES = 8 * 1024 * 1024
ATTN_QUERY_BLOCK = 2048
ATTN_MAX_SEQS = 4
MASK_DIST = 1e33


def _silu(x):
    return x * jax.nn.sigmoid(x)


def _adaln_norm(x, g, shift, scale):
    y = x * lax.rsqrt(jnp.mean(x * x, axis=-1, keepdims=True) + EPS)
    return (y * g) * (1.0 + scale) + shift


def _const_spec(shape):
    nd = len(shape)
    return pl.BlockSpec(shape, lambda *_: (0,) * nd, pipeline_mode=pl.Buffered(1))


def _layer_spec(stacked, layer):
    shape = stacked.shape[1:]
    nd = len(shape)
    return pl.BlockSpec((None,) + shape, lambda *_: (layer,) + (0,) * nd, pipeline_mode=pl.Buffered(1))


def _params(n_grid):
    return pltpu.CompilerParams(dimension_semantics=("arbitrary",) * n_grid,
                                vmem_limit_bytes=VMEM_LIMIT)


def _mod_spec(mod, nb, t):
    _, mt, d3 = mod.shape
    if mt == 1:
        return pl.BlockSpec((nb, 1, d3), lambda i, j: (i, 0, 0))
    return pl.BlockSpec((nb, t, d3), lambda i, j: (i, j, 0))


def _mod_body(c_ref, w_ref, b_ref, o_ref):
    a = _silu(c_ref[...]).astype(BF16)
    o_ref[...] = jnp.dot(a, w_ref[...].astype(BF16), preferred_element_type=F32) + b_ref[...]


def _adaln_mod(c_all, ada_w, ada_b):
    r, d = c_all.shape
    depth, _, d3 = ada_w.shape
    tn = N_CHUNK
    return pl.pallas_call(
        _mod_body,
        grid=(depth, d3 // tn),
        in_specs=[pl.BlockSpec((r, d), lambda i, n: (0, 0)),
                  pl.BlockSpec((None, d, tn), lambda i, n: (i, 0, n)),
                  pl.BlockSpec((None, 1, tn), lambda i, n: (i, 0, n))],
        out_specs=pl.BlockSpec((None, r, tn), lambda i, n: (i, 0, n)),
        out_shape=jax.ShapeDtypeStruct((depth, r, d3), F32),
        compiler_params=_params(2),
        name="adaln_mod",
    )(c_all, ada_w, ada_b.reshape(depth, 1, d3))


def _lru_body(x_ref, mod_ref, g_ref, win_ref, cw_ref, cb_ref, wg_ref, ba_ref, bx_ref, lam_ref,
              wout_ref, cs0_ref, h0_ref, y_ref, cs_ref, ht_ref, prev_sc, h_sc):
    nb, t, d = x_ref.shape
    taps, r = cw_ref.shape
    assert taps == 4, "the conv below pairs its taps two by two"
    gw = wg_ref.shape[1]

    @pl.when(pl.program_id(1) == 0)
    def _():
        prev_sc[...] = cs0_ref[...]
        h_sc[...] = h0_ref[...]

    x = x_ref[...]
    mod = mod_ref[...]
    shift, scale, gate = mod[..., :d], mod[..., d:2 * d], mod[..., 2 * d:]
    hn = _adaln_norm(x, g_ref[...], shift, scale).reshape(nb * t, d).astype(BF16)
    proj = jnp.dot(hn, win_ref[...], preferred_element_type=F32)
    xb = proj[:, :r].reshape(nb, t, r)
    gb = proj[:, r:]

    cat = jnp.concatenate([prev_sc[...], xb], axis=1)
    cw = cw_ref[...]
    back1 = pltpu.roll(cat, 1, 1)
    older = pltpu.roll(cat * cw[1:2] + back1 * cw[0:1], 2, 1)
    xc = cb_ref[...] + xb * cw[3:4] + back1[:, SUBLANES:, :] * cw[2:3] + older[:, SUBLANES:, :]
    new_prev = cat[:, t:, :]
    prev_sc[...] = new_prev
    cs_ref[...] = new_prev

    xc2 = xc.reshape(nb * t, r)
    xcb = xc2.astype(BF16)
    grs, gis = [], []
    for q in range(r // gw):
        p = jnp.dot(xcb[:, q * gw:(q + 1) * gw], wg_ref[q], preferred_element_type=F32)
        grs.append(p[:, :gw])
        gis.append(p[:, gw:])
    rg = jax.nn.sigmoid(jnp.concatenate(grs, axis=1) + ba_ref[...])
    ig = jax.nn.sigmoid(jnp.concatenate(gis, axis=1) + bx_ref[...])
    nlam = -lam_ref[...]
    softplus = jnp.maximum(nlam, 0.0) + jnp.log1p(jnp.exp(-jnp.abs(nlam)))
    z = LRU_C * rg * softplus
    a = jnp.exp(-z)
    u = jnp.sqrt(jnp.tanh(z) * (1.0 + a * a)) * ig * xc2

    groups = t // SUBLANES
    a3 = a.reshape(nb * groups, SUBLANES, r)
    u3 = u.reshape(nb * groups, SUBLANES, r)
    row = lax.broadcasted_iota(jnp.int32, (1, SUBLANES, r), 1)
    dist = 1
    while dist < SUBLANES:
        keep = row >= dist
        u3 = jnp.where(keep, a3 * pltpu.roll(u3, dist, 1) + u3, u3)
        a3 = jnp.where(keep, a3 * pltpu.roll(a3, dist, 1), a3)
        dist *= 2
    a4 = a3.reshape(nb, groups, SUBLANES, r)
    u4 = u3.reshape(nb, groups, SUBLANES, r)
    h_prev = h_sc[...]
    hs = []
    for gi in range(groups):
        hs.append(a4[:, gi] * h_prev + u4[:, gi])
        h_prev = hs[-1][:, SUBLANES - 1:, :]
    h = hs[0] if groups == 1 else jnp.concatenate(hs, axis=1)
    h_last = h_prev
    h_sc[...] = h_last
    ht_ref[...] = h_last

    y = (h.reshape(nb * t, r) * _silu(gb)).astype(BF16)
    out = jnp.dot(y, wout_ref[...], preferred_element_type=F32).reshape(nb, t, d)
    y_ref[...] = x + gate * out


def _gate_weights(wa, wx):
    nblk, bw, _ = wa.shape
    ngrp = nblk // GATE_GROUP
    eye = jnp.eye(GATE_GROUP, dtype=wa.dtype)[None, :, None, :, None]

    def block_diag(w):
        w = w.reshape(ngrp, GATE_GROUP, bw, 1, bw) * eye
        return w.reshape(ngrp, GATE_GROUP * bw, GATE_GROUP * bw)

    return jnp.concatenate([block_diag(wa), block_diag(wx)], axis=-1).astype(BF16)


def _lru_layer(x, mod, layer, norm_g, w_in, conv_w, conv_b, wg, ba, bx, lam, w_out, cs0, h0, nb, t):
    b, s, d = x.shape
    r = conv_w.shape[1]
    gw = wg.shape[1]
    row = lambda v: v.reshape(1, -1)
    blk = lambda i, j: (i, j, 0)
    per_b = lambda i, j: (i, 0, 0)
    return pl.pallas_call(
        _lru_body,
        grid=(b // nb, s // t),
        in_specs=[pl.BlockSpec((nb, t, d), blk),
                  pl.BlockSpec((nb, 1, 3 * d), per_b),
                  _const_spec((1, d)),
                  _layer_spec(w_in, layer),
                  _const_spec((CONV_W, r)),
                  _const_spec((1, r)),
                  _const_spec((r // gw, gw, 2 * gw)),
                  _const_spec((1, r)),
                  _const_spec((1, r)),
                  _const_spec((1, r)),
                  _layer_spec(w_out, layer),
                  pl.BlockSpec((nb, SUBLANES, r), per_b),
                  pl.BlockSpec((nb, 1, r), per_b)],
        out_specs=[pl.BlockSpec((nb, t, d), blk),
                   pl.BlockSpec((nb, SUBLANES, r), per_b),
                   pl.BlockSpec((nb, 1, r), per_b)],
        out_shape=[jax.ShapeDtypeStruct((b, s, d), F32),
                   jax.ShapeDtypeStruct((b, SUBLANES, r), F32),
                   jax.ShapeDtypeStruct((b, 1, r), F32)],
        scratch_shapes=[pltpu.VMEM((nb, SUBLANES, r), F32), pltpu.VMEM((nb, 1, r), F32)],
        compiler_params=_params(2),
        name="lru_layer",
    )(x, mod, row(norm_g), w_in, conv_w, row(conv_b), wg, row(ba), row(bx), row(lam), w_out, cs0, h0)


def _attproj_body(x_ref, mod_ref, g_ref, w_ref, *refs, att_scale):
    p0_ref, p1_ref, p2_ref, gate_ref, t0_ref, t1_ref, t2_ref, stage_sc = refs[-8:]
    _, t, d = x_ref.shape
    mod = mod_ref[...]
    hn = _adaln_norm(x_ref[...], g_ref[...], mod[..., :d], mod[..., d:2 * d])
    hn = hn.reshape(t, d).astype(BF16)
    tails = (t0_ref, t1_ref, t2_ref)
    qkvs = (p0_ref, p1_ref, p2_ref)
    for c in range(3 * N_GROUPS + 1):
        res = jnp.dot(hn, w_ref[:, c * N_CHUNK:(c + 1) * N_CHUNK], preferred_element_type=F32)
        g, j = divmod(c, 3)
        if g == N_GROUPS:
            gate_ref[...] = res.astype(gate_ref.dtype)
            continue
        if j == 0:
            res = res * att_scale
        else:
            rows = tails[g].shape[0]
            tails[g][:, (j - 1) * N_CHUNK:j * N_CHUNK] = res[t - rows:, :]
        cols = slice(j * N_CHUNK, (j + 1) * N_CHUNK)
        dil = qkvs[g].shape[0]
        if dil == 1:
            qkvs[g][0, :, cols] = res.astype(qkvs[g].dtype)
        else:
            nblk = N_CHUNK // LANES
            for cb in range(nblk):
                stage_sc[cb] = res[:, cb * LANES:(cb + 1) * LANES]
            for r in range(dil):
                picked = [stage_sc[cb, pl.ds(r, t // dil, stride=dil), :] for cb in range(nblk)]
                qkvs[g][r, :, cols] = jnp.concatenate(picked, axis=1).astype(qkvs[g].dtype)


def _att_proj(x, mod, norm_g, w_in, layer, t, dils, tail_lens, p_dtype, tails_so_far=None):
    b, s, d = x.shape
    n_layers = w_in.shape[0]
    n_t = s // t
    qkv_shapes = [jax.ShapeDtypeStruct((b, dil, s // dil, 3 * N_CHUNK), p_dtype) for dil in dils]
    qkv_specs = [pl.BlockSpec((None, dil, t // dil, 3 * N_CHUNK), lambda i, j: (i, 0, j, 0)) for dil in dils]
    tail_shapes, tail_specs = [], []
    for g in range(N_GROUPS):
        wlen = tail_lens[g]
        rows = min(wlen, t)
        first = n_t - wlen // rows
        tail_shapes.append(jax.ShapeDtypeStruct((n_layers, b, wlen, 2 * N_CHUNK), F32))
        tail_specs.append(pl.BlockSpec((None, None, rows, 2 * N_CHUNK),
                                       lambda i, j, first=first: (layer, i, jnp.maximum(j - first, 0), 0)))
    head_dim = d // N_HEADS
    inputs = [x, mod, norm_g.reshape(1, -1), w_in]
    in_specs = [pl.BlockSpec((1, t, d), lambda i, j: (i, j, 0)),
                _mod_spec(mod, 1, t),
                _const_spec((1, d)),
                _layer_spec(w_in, layer)]
    aliases = {}
    if tails_so_far is not None:
        n_out_before_tails = len(dils) + 1
        for g, tail in enumerate(tails_so_far):
            aliases[len(inputs)] = n_out_before_tails + g
            inputs.append(tail)
            in_specs.append(pl.BlockSpec(memory_space=pl.ANY))
    return pl.pallas_call(
        functools.partial(_attproj_body, att_scale=head_dim ** -0.5),
        grid=(b, n_t),
        in_specs=in_specs,
        out_specs=qkv_specs + [pl.BlockSpec((None, t, N_CHUNK), lambda i, j: (i, j, 0))] + tail_specs,
        out_shape=qkv_shapes + [jax.ShapeDtypeStruct((b, s, N_CHUNK), p_dtype)] + tail_shapes,
        scratch_shapes=[pltpu.VMEM((N_CHUNK // LANES, t, LANES), F32)],
        input_output_aliases=aliases,
        compiler_params=_params(2),
        name="att_proj",
    )(*inputs)


def _slope(h):
    return 2.0 ** (-8.0 * (h + 1) / N_HEADS)


def _stat_lane(h):
    return h


def _pattn_body(q_ref, kh_ref, kc_ref, vh_ref, vc_ref, o_ref, m_ref, l_ref, k_sc, v_sc, pen_sc, *, dil, n):
    rb, qb, _ = q_ref.shape

    @pl.when((pl.program_id(0) == 0) & (pl.program_id(1) == 0) & (pl.program_id(2) == 0))
    def _():
        ii = lax.broadcasted_iota(jnp.int32, (n, 2 * n), 0)
        jj = lax.broadcasted_iota(jnp.int32, (n, 2 * n), 1)
        diff = ii - jj + n
        dist = jnp.where((diff >= 0) & (diff <= n), (diff * dil).astype(F32), MASK_DIST)
        dist_first = jnp.where(jj >= n, dist, MASK_DIST)
        for h in range(N_HEADS):
            pen_sc[0, h] = _slope(h) * dist
            pen_sc[1, h] = _slope(h) * dist_first

    k_sc[:, :n, :] = kh_ref[...]
    k_sc[:, n:, :] = kc_ref[...]
    v_sc[:, :n, :] = vh_ref[...]
    v_sc[:, n:, :] = vc_ref[...]
    lane = lax.broadcasted_iota(jnp.int32, (n, LANES), 1)
    low = lane < (LANES // 2)
    ones = jnp.ones((2 * n, LANES), BF16)
    nt = (((1,), (1,)), ((), ()))
    for ri, sc in [(ri, sc) for ri in range(rb) for sc in range(qb // n)]:
        r0 = sc * n
        table = jnp.where(pl.program_id(2) > 0, 0, 1) if sc == 0 else 0
        m_all = jnp.zeros((n, LANES), F32)
        l_all = jnp.ones((n, LANES), F32)
        for p in range(N_HEADS // 2):
            cols = slice(p * LANES, (p + 1) * LANES)
            q2 = q_ref[ri, r0:r0 + n, cols]
            k2 = k_sc[ri, r0:r0 + 2 * n, cols]
            v2 = jnp.concatenate([v_sc[ri, r0:r0 + 2 * n, cols], ones], axis=1)
            outs = []
            for half in range(2):
                h = 2 * p + half
                qm = jnp.where(low if half == 0 else ~low, q2, jnp.zeros_like(q2))
                s = lax.dot_general(qm, k2, nt, preferred_element_type=F32) - pen_sc[table, h]
                m = jnp.max(s, axis=-1, keepdims=True)
                e = jnp.exp(s - m).astype(BF16)
                ov = jnp.dot(e, v2, preferred_element_type=F32)
                outs.append(ov[:, :LANES])
                m_all = jnp.where(lane == h, m, m_all)
                l_all = jnp.where(lane == h, ov[:, LANES:], l_all)
            o_ref[ri, r0:r0 + n, cols] = jnp.where(low, outs[0], outs[1]).astype(o_ref.dtype)
        m_ref[ri, r0:r0 + n, :] = m_all
        l_ref[ri, r0:r0 + n, :] = l_all


def _prompt_attn(qkv, g):
    b, dil, steps, _ = qkv.shape
    n = WINDOWS[g] // dil
    qb = min(steps, ATTN_QUERY_BLOCK)
    rb = min(dil, ATTN_QUERY_BLOCK // qb, ATTN_MAX_SEQS)
    cur = lambda col: (lambda i, r, c: (i, r, c, col))
    halo = lambda col: (lambda i, r, c: (i, r, jnp.maximum(c * (qb // n) - 1, 0), col))
    cur_spec = lambda col: pl.BlockSpec((None, rb, qb, N_CHUNK), cur(col))
    halo_spec = lambda col: pl.BlockSpec((None, rb, n, N_CHUNK), halo(col))
    return pl.pallas_call(
        functools.partial(_pattn_body, dil=dil, n=n),
        grid=(b, dil // rb, steps // qb),
        in_specs=[cur_spec(0), halo_spec(1), cur_spec(1), halo_spec(2), cur_spec(2)],
        out_specs=[pl.BlockSpec((None, rb, qb, N_CHUNK), cur(0)),
                   pl.BlockSpec((None, rb, qb, LANES), cur(0)),
                   pl.BlockSpec((None, rb, qb, LANES), cur(0))],
        out_shape=[jax.ShapeDtypeStruct((b, dil, steps, N_CHUNK), BF16),
                   jax.ShapeDtypeStruct((b, dil, steps, LANES), F32),
                   jax.ShapeDtypeStruct((b, dil, steps, LANES), F32)],
        scratch_shapes=[pltpu.VMEM((rb, qb + n, N_CHUNK), BF16), pltpu.VMEM((rb, qb + n, N_CHUNK), BF16),
                        pltpu.VMEM((2, N_HEADS, n, 2 * n), F32)],
        compiler_params=_params(3),
        name=f"prompt_attn_g{g}",
    )(qkv, qkv, qkv, qkv, qkv)


def _sattn_body(p_ref, cache_ref, o_ref, m_ref, l_ref, m_sc, l_sc, acc_sc, *, dil, window):
    nb, t, _ = p_ref.shape
    pc = cache_ref.shape[2]
    hd = N_CHUNK // N_HEADS
    rows = N_HEADS * t
    j = pl.program_id(1)
    buf_len = pc * pl.num_programs(1)
    nt = (((1,), (1,)), ((), ()))

    own_head = (lax.broadcasted_iota(jnp.int32, (rows, N_CHUNK), 0) // t
                == lax.broadcasted_iota(jnp.int32, (rows, N_CHUNK), 1) // hd)
    ri = lax.broadcasted_iota(jnp.int32, (rows, 1), 0)
    head = ri // t
    tq = ri - head * t
    slope = jnp.exp((head + 1).astype(F32) * (-8.0 / N_HEADS * math.log(2.0)))

    def block_diag_q(bi):
        q = p_ref[bi, :, :N_CHUNK]
        qbd = jnp.where(own_head, jnp.broadcast_to(q[None], (N_HEADS, t, N_CHUNK)).reshape(rows, N_CHUNK), 0.0)
        return qbd.astype(BF16)

    qbds = [block_diag_q(bi) for bi in range(nb)]

    @pl.when(j == 0)
    def _():
        pad = jnp.zeros((LANES - t, N_CHUNK), F32)
        dist = tq - lax.broadcasted_iota(jnp.int32, (rows, LANES), 1)
        ok = (dist >= 0) & ((dist & (dil - 1)) == 0)
        pen = slope * dist.astype(F32)
        for bi in range(nb):
            kn = jnp.concatenate([p_ref[bi, :, N_CHUNK:2 * N_CHUNK], pad], axis=0).astype(BF16)
            vn = jnp.concatenate([p_ref[bi, :, 2 * N_CHUNK:], pad], axis=0).astype(BF16)
            s = lax.dot_general(qbds[bi], kn, nt, preferred_element_type=F32)
            s = jnp.where(ok, s - pen, NEG_INF)
            m = jnp.max(s, axis=-1, keepdims=True)
            e = jnp.exp(s - m)
            m_sc[bi] = m
            l_sc[bi] = jnp.sum(e, axis=-1, keepdims=True)
            acc_sc[bi] = jnp.dot(e.astype(BF16), vn, preferred_element_type=F32)

    pos = j * pc + lax.broadcasted_iota(jnp.int32, (rows, pc), 1)
    dist = buf_len + tq - pos
    ok = ((dist & (dil - 1)) == 0) & (dist <= window)
    pen = slope * dist.astype(F32)
    for bi in range(nb):
        kt = cache_ref[bi, :N_CHUNK, :].astype(BF16)
        vt = cache_ref[bi, N_CHUNK:, :].astype(BF16)
        s = jnp.dot(qbds[bi], kt, preferred_element_type=F32)
        s = jnp.where(ok, s - pen, NEG_INF)
        m_old = m_sc[bi]
        m_new = jnp.maximum(m_old, jnp.max(s, axis=-1, keepdims=True))
        alpha = jnp.exp(m_old - m_new)
        e = jnp.exp(s - m_new)
        l_sc[bi] = alpha * l_sc[bi] + jnp.sum(e, axis=-1, keepdims=True)
        acc_sc[bi] = alpha * acc_sc[bi] + lax.dot_general(e.astype(BF16), vt, nt, preferred_element_type=F32)
        m_sc[bi] = m_new

    @pl.when(j == pl.num_programs(1) - 1)
    def _():
        on_lane = (_stat_lane(lax.broadcasted_iota(jnp.int32, (rows, LANES), 0) // t)
                   == lax.broadcasted_iota(jnp.int32, (rows, LANES), 1))
        fold = lambda v: jnp.sum(v.reshape(N_HEADS, t, v.shape[-1]), axis=0)
        used = fold(on_lane.astype(F32))
        for bi in range(nb):
            o_ref[bi] = fold(jnp.where(own_head, acc_sc[bi], 0.0))
            m_ref[bi] = fold(jnp.where(on_lane, m_sc[bi], 0.0))
            l_ref[bi] = fold(jnp.where(on_lane, l_sc[bi], 0.0)) + (1.0 - used)


def _sample_attn(qkv, cache, layer, g):
    b, t, _ = qkv.shape
    natt, _, blen = cache.shape[:3]
    cache_t = jnp.transpose(cache, (0, 1, 3, 4, 5, 2)).reshape(natt, b, 2 * N_CHUNK, blen)
    pc = min(blen, SAMPLE_POS_CHUNK)
    nb = max(1, min(b, SAMPLE_CACHE_BLOCK_BYTES // (2 * N_CHUNK * pc * 4)))
    per_b = lambda i, j: (i, 0, 0)
    return pl.pallas_call(
        functools.partial(_sattn_body, dil=DILATIONS[g], window=WINDOWS[g]),
        grid=(b // nb, blen // pc),
        in_specs=[pl.BlockSpec((nb, t, 3 * N_CHUNK), per_b),
                  pl.BlockSpec((None, nb, 2 * N_CHUNK, pc), lambda i, j: (layer, i, 0, j))],
        out_specs=[pl.BlockSpec((nb, t, N_CHUNK), per_b),
                   pl.BlockSpec((nb, t, LANES), per_b),
                   pl.BlockSpec((nb, t, LANES), per_b)],
        out_shape=[jax.ShapeDtypeStruct((b, t, N_CHUNK), F32),
                   jax.ShapeDtypeStruct((b, t, LANES), F32),
                   jax.ShapeDtypeStruct((b, t, LANES), F32)],
        scratch_shapes=[pltpu.VMEM((nb, N_HEADS * t, 1), F32), pltpu.VMEM((nb, N_HEADS * t, 1), F32),
                        pltpu.VMEM((nb, N_HEADS * t, N_CHUNK), F32)],
        compiler_params=_params(2),
        name=f"sample_attn_g{g}",
    )(qkv, cache_t)


def _natural_rows(ref, stage_sc):
    dil, n, c = ref.shape
    if dil == 1:
        return ref[0].astype(F32)
    nblk = c // LANES
    for r in range(dil):
        rows = ref[r].astype(F32)
        for cb in range(nblk):
            stage_sc[cb, pl.ds(r, n, stride=dil), :] = rows[:, cb * LANES:(cb + 1) * LANES]
    return jnp.concatenate([stage_sc[cb] for cb in range(nblk)], axis=1)


def _attout_body(x_ref, mod_ref, gate_ref, o0_ref, o1_ref, o2_ref, m0_ref, m1_ref, m2_ref,
                 l0_ref, l1_ref, l2_ref, wout_ref, fg_ref, y_ref, stage_sc, *, final_norm):
    _, t, d = x_ref.shape
    hd = d // N_HEADS
    ms = [_natural_rows(r, stage_sc) for r in (m0_ref, m1_ref, m2_ref)]
    ls = [_natural_rows(r, stage_sc) for r in (l0_ref, l1_ref, l2_ref)]
    m = jnp.maximum(jnp.maximum(ms[0], ms[1]), ms[2])
    es = [jnp.exp(mg - m) for mg in ms]
    den = es[0] * ls[0] + es[1] * ls[1] + es[2] * ls[2]
    expand = (lax.broadcasted_iota(jnp.int32, (2 * LANES, d), 0) % LANES
              == _stat_lane(lax.broadcasted_iota(jnp.int32, (2 * LANES, d), 1) // hd)).astype(BF16)
    acc = jnp.zeros((t, d), F32)
    for e, o_ref in zip(es, (o0_ref, o1_ref, o2_ref)):
        w = e / den
        hi = w.astype(BF16)
        lo = (w - hi.astype(F32)).astype(BF16)
        wfull = jnp.dot(jnp.concatenate([hi, lo], axis=1), expand, preferred_element_type=F32)
        acc = acc + wfull * _natural_rows(o_ref, stage_sc)
    y = (acc * _silu(gate_ref[...].astype(F32))).astype(BF16)
    out = jnp.dot(y, wout_ref[...], preferred_element_type=F32)
    xn = x_ref[0] + mod_ref[0][:, 2 * d:] * out
    if final_norm:
        xn = xn * lax.rsqrt(jnp.mean(xn * xn, axis=-1, keepdims=True) + EPS) * fg_ref[...]
    y_ref[0] = xn


def _att_out(x, mod, gate, os_, ms, ls, w_out, layer, final_g, final_norm, t):
    b, s, d = x.shape
    blk = lambda i, j: (i, j, 0)
    res_spec = lambda v: pl.BlockSpec((None, v.shape[1], t // v.shape[1], v.shape[3]), lambda i, j: (i, 0, j, 0))
    return pl.pallas_call(
        functools.partial(_attout_body, final_norm=final_norm),
        grid=(b, s // t),
        in_specs=[pl.BlockSpec((1, t, d), blk),
                  _mod_spec(mod, 1, t),
                  pl.BlockSpec((None, t, N_CHUNK), blk)]
                 + [res_spec(v) for v in (*os_, *ms, *ls)]
                 + [_layer_spec(w_out, layer), _const_spec((1, d))],
        out_specs=pl.BlockSpec((1, t, d), blk),
        out_shape=jax.ShapeDtypeStruct((b, s, d), F32),
        scratch_shapes=[pltpu.VMEM((N_CHUNK // LANES, t, LANES), F32)],
        compiler_params=_params(2),
        name="att_out",
    )(x, mod, gate, *os_, *ms, *ls, w_out, final_g.reshape(1, -1))


def kernel(x_prompt, x_sample, state_conv, state_h, cache_kv_g0, cache_kv_g1, cache_kv_g2, c_prompt, c_sample, norm_g, ada_w, ada_b, final_g, lru_w_in, lru_conv_w, lru_conv_b, lru_wa, lru_ba, lru_wx, lru_bx, lru_lambda, lru_w_out, att_w_in, att_w_out):
    bp, sp, d = x_prompt.shape
    bs, ts, _ = x_sample.shape
    r = lru_conv_w.shape[-1]
    hd = d // N_HEADS
    caches = (cache_kv_g0, cache_kv_g1, cache_kv_g2)

    n_c = bp + bs
    c_all = jnp.concatenate([c_prompt, c_sample, jnp.zeros((-n_c % 16, d), F32)], axis=0)
    mod = _adaln_mod(c_all, ada_w, ada_b)
    mod_p = mod[:, :bp].reshape(DEPTH, bp, 1, 3 * d)
    mod_s = mod[:, bp:n_c].reshape(DEPTH, bs, 1, 3 * d)

    lru_w_in, lru_w_out, att_w_in, att_w_out = (w.astype(BF16) for w in (lru_w_in, lru_w_out, att_w_in, att_w_out))
    yp, ys = x_prompt, x_sample
    p_conv, p_h, s_conv, s_h = [], [], [], []
    p_kv = s_kv = None
    n_keep = CONV_W - 1
    for i in range(DEPTH):
        j = i // N_MIXERS
        if i % N_MIXERS == 0:
            wg = _gate_weights(lru_wa[j], lru_wx[j])
            args = (j, norm_g[i], lru_w_in, lru_conv_w[j], lru_conv_b[j], wg, lru_ba[j],
                    lru_bx[j], lru_lambda[j], lru_w_out)
            zc = jnp.zeros((bp, SUBLANES, r), F32)
            zh = jnp.zeros((bp, 1, r), F32)
            yp, cs, ht = _lru_layer(yp, mod_p[i], *args, zc, zh, 1, LRU_TILE)
            p_conv.append(cs[:, SUBLANES - n_keep:])
            p_h.append(ht[:, 0])
            cs0 = jnp.pad(state_conv[j], ((0, 0), (SUBLANES - n_keep, 0), (0, 0)))
            ys, cs, ht = _lru_layer(ys, mod_s[i], *args, cs0, state_h[j][:, None, :], bs, ts)
            s_conv.append(cs[:, SUBLANES - n_keep:])
            s_h.append(ht[:, 0])
        else:
            last = i == DEPTH - 1
            p_tails = [min(w, sp) for w in WINDOWS]
            *qkvs, gate, t0, t1, t2 = _att_proj(yp, mod_p[i], norm_g[i], att_w_in, j, PROMPT_TILE, DILATIONS,
                                                p_tails, BF16, p_kv)
            p_kv = (t0, t1, t2)
            res = [_prompt_attn(qkvs[g], g) for g in range(N_GROUPS)]
            yp = _att_out(yp, mod_p[i], gate, *zip(*res), att_w_out, j, final_g, last, ATT_OUT_TILE)
            n_s = bs * ts
            flat = lambda v: v.reshape(1, n_s, v.shape[-1])
            mod_rows = flat(jnp.broadcast_to(mod_s[i], (bs, ts, 3 * d)))
            *qkvs, gate, t0, t1, t2 = _att_proj(flat(ys), mod_rows, norm_g[i], att_w_in, j, n_s,
                                                (1,) * N_GROUPS, [n_s] * N_GROUPS, F32, s_kv)
            s_kv = (t0, t1, t2)
            res = [_sample_attn(qkvs[g].reshape(bs, ts, -1), caches[g], j, g) for g in range(N_GROUPS)]
            nat = lambda v: v.reshape(1, 1, n_s, v.shape[-1])
            res = [[nat(v) for v in group] for group in res]
            ys = _att_out(flat(ys), mod_rows, gate, *zip(*res), att_w_out, j, final_g, last, n_s).reshape(bs, ts, d)
    natt = att_w_in.shape[0]
    p_kv = [tail.reshape(natt, bp, -1, 2, N_HEADS, hd) for tail in p_kv]
    s_kv = [tail.reshape(natt, bs, ts, 2, N_HEADS, hd) for tail in s_kv]
    return (yp, ys, jnp.stack(p_conv), jnp.stack(p_h), *p_kv, jnp.stack(s_conv), jnp.stack(s_h), *s_kv)
```
